```python
import jax, jax.numpy as jnp
from jax import lax
import numpy as np

D_MODEL = 2048
BATCH = 2
SEQ = 4096
DEPTH = 4

GRID_W = 64
CTX_LEN = 256
EPS = 1e-6
N_MOD = 6
D_FF = 4 * D_MODEL
N_EVEN = (DEPTH + 1) // 2
N_ODD = DEPTH // 2

D_LRU = D_MODEL // 2
LRU_HEADS = 8
LRU_HEAD_DIM = D_LRU // LRU_HEADS
CONV_W = 4
CONV_LEFT = CONV_W // 2
LRU_C = 8.0

MLA_HEADS = 8
QK_NOPE = 128
QK_ROPE = 64
V_HEAD = 128
Q_LORA = 512
KV_LORA = 512
ROPE_BASE = 10000.0
Q_BLOCK = 128
ATTN_SCALE = (QK_NOPE + QK_ROPE) ** -0.5
D_MLA_OUT = MLA_HEADS * V_HEAD
D_AB_IN = 2 * D_LRU + Q_LORA + KV_LORA + QK_ROPE
D_AB_OUT = D_LRU + D_MLA_OUT

HG_EXPAND = 128
HG_HEADS = D_MODEL // HG_EXPAND
HG_DK = HG_EXPAND
HG_DV = D_MODEL // HG_HEADS
D_HG = HG_HEADS * HG_DK
D_HG_V = HG_HEADS * HG_DV
D_HG_IN = 3 * D_HG + 2 * D_HG_V
HG_CHUNK = 64

kernel_name = "hybrid_rglru_mla_hgrn2_diffusion_block"


def _same(t):
    return t


def _flip_seq(t):
    return jnp.flip(t, axis=1)


def rms_norm(x, w):
    xf = x.astype(jnp.float32)
    y = xf * lax.rsqrt(jnp.mean(xf * xf, axis=-1, keepdims=True) + EPS)
    return (y * w.astype(jnp.float32)).astype(x.dtype)


def modulate(h, shift, scale):
    return h * (1 + scale) + shift


def ada_params(cond, w, b):
    m = jnp.einsum('...d,de->...e', jax.nn.silu(cond), w) + b
    return jnp.split(m[..., None, :], N_MOD, axis=-1)


def squared_relu_mlp(h, w1, w2):
    return jnp.square(jax.nn.relu(h @ w1)) @ w2


def axial_rope_tables(row, col):
    half = QK_ROPE // 2
    inv_freq = ROPE_BASE ** (-jnp.arange(0, half, 2, dtype=jnp.float32) / half)
    ang_r = row.astype(jnp.float32)[:, None] * inv_freq
    ang_c = col.astype(jnp.float32)[:, None] * inv_freq
    ang = jnp.concatenate([ang_r, ang_r, ang_c, ang_c], axis=-1)
    return jnp.cos(ang), jnp.sin(ang)


def apply_rope(x, cos, sin):
    q = QK_ROPE // 4
    rot = jnp.concatenate([-x[..., q:2 * q], x[..., :q], -x[..., 3 * q:], x[..., 2 * q:3 * q]], axis=-1)
    return (x * cos + rot * sin).astype(x.dtype)


def dwconv_centred(u, w, b):
    T = u.shape[1]
    up = jnp.pad(u, ((0, 0), (CONV_LEFT, CONV_W - 1 - CONV_LEFT), (0, 0)))
    taps = [up[:, j:j + T] * w[j] for j in range(CONV_W)]
    return sum(taps[1:], taps[0]) + b


def linear_scan(a, b, h0):
    b = b.at[:, 0].add(a[:, 0] * h0)

    def combine(l, r):
        al, bl = l
        ar, br = r
        return al * ar, ar * bl + br

    _, h = lax.associative_scan(combine, (a, b), axis=1)
    return h


def rglru_gates(u, w_a, b_a, w_x, b_x, lam):
    uh = u.reshape(u.shape[:-1] + (LRU_HEADS, LRU_HEAD_DIM))
    r = jax.nn.sigmoid(jnp.einsum('bthd,hde->bthe', uh, w_a).reshape(u.shape) + b_a)
    i = jax.nn.sigmoid(jnp.einsum('bthd,hde->bthe', uh, w_x).reshape(u.shape) + b_x)
    log_a = -LRU_C * r * jax.nn.softplus(-lam)
    a = jnp.exp(log_a)
    b = jnp.sqrt(-jnp.expm1(2.0 * log_a)) * (i * u)
    return a, b


def rglru_bidir(uc, ux, w_a, b_a, w_x, b_x, lam, need_ctx):
    hc_dirs, hx_dirs = [], []
    for d in range(2):
        rev = _flip_seq if d == 1 else _same
        a, b = rglru_gates(rev(uc), w_a[d], b_a[d], w_x[d], b_x[d], lam[d])
        h_c = linear_scan(a, b, jnp.zeros_like(b[:, 0]))
        a, b = rglru_gates(rev(ux), w_a[d], b_a[d], w_x[d], b_x[d], lam[d])
        h_x = linear_scan(a, b, h_c[:, -1])
        hx_dirs.append(rev(h_x))
        if need_ctx:
            hc_dirs.append(rev(h_c))
    h_c_sum = hc_dirs[0] + hc_dirs[1] if need_ctx else None
    return h_c_sum, hx_dirs[0] + hx_dirs[1]


def mla_queries(pq, norm_w, w_uq):
    B, T = pq.shape[:2]
    q = (rms_norm(pq, norm_w) @ w_uq).reshape(B, T, MLA_HEADS, QK_NOPE + QK_ROPE)
    return q[..., :QK_NOPE], q[..., QK_NOPE:]


def mla_keys_values(pkv, norm_w, w_ukv):
    B, T = pkv.shape[:2]
    kv = (rms_norm(pkv, norm_w) @ w_ukv).reshape(B, T, MLA_HEADS, QK_NOPE + V_HEAD)
    return kv[..., :QK_NOPE], kv[..., QK_NOPE:]


def blocked_attention(qn, qr, kn, kr, v):
    B, T = qn.shape[:2]
    nb = T // Q_BLOCK

    def blocks(t):
        return jnp.swapaxes(t.reshape((B, nb, Q_BLOCK) + t.shape[2:]), 0, 1)

    def attend(blk):
        qn_i, qr_i = blk
        s = jnp.einsum('bqhd,bkhd->bhqk', qn_i, kn) + jnp.einsum('bqhr,bkr->bhqk', qr_i, kr)
        p = jax.nn.softmax(s.astype(jnp.float32) * ATTN_SCALE, axis=-1).astype(v.dtype)
        return jnp.einsum('bhqk,bkhd->bqhd', p, v)

    o = lax.map(attend, (blocks(qn), blocks(qr)))
    return jnp.swapaxes(o, 0, 1).reshape(B, T, D_MLA_OUT)


def split_ab(p):
    return jnp.split(p, [D_LRU, 2 * D_LRU, 2 * D_LRU + Q_LORA, 2 * D_LRU + Q_LORA + KV_LORA], axis=-1)


def mixer_ab(hc, hx, cos, sin, w_in, w_out, conv_w, conv_b, w_a, b_a, w_x, b_x, lam,
             q_norm_w, w_uq, kv_norm_w, w_ukv, need_ctx):
    gate_c, u_c, cq_c, ckv_c, kr_c = split_ab(hc @ w_in)
    gate_x, u_x, cq_x, ckv_x, kr_x = split_ab(hx @ w_in)
    uc = dwconv_centred(u_c, conv_w, conv_b).astype(jnp.float32)
    ux = dwconv_centred(u_x, conv_w, conv_b).astype(jnp.float32)
    h_c, h_x = rglru_bidir(uc, ux, w_a, b_a, w_x, b_x, lam, need_ctx)
    ya_x = (h_x * jax.nn.gelu(gate_x.astype(jnp.float32))).astype(hx.dtype)
    kn_c, v_c = mla_keys_values(ckv_c, kv_norm_w, w_ukv)
    kn_x, v_x = mla_keys_values(ckv_x, kv_norm_w, w_ukv)
    qn_x, qr_x = mla_queries(cq_x, q_norm_w, w_uq)
    qr_x = apply_rope(qr_x, cos[:, None], sin[:, None])
    kr_x = apply_rope(kr_x, cos, sin)
    yb_x = blocked_attention(qn_x, qr_x,
                             jnp.concatenate([kn_x, kn_c], axis=1),
                             jnp.concatenate([kr_x, kr_c], axis=1),
                             jnp.concatenate([v_x, v_c], axis=1))
    out_x = jnp.concatenate([ya_x, yb_x], axis=-1) @ w_out
    if not need_ctx:
        return None, out_x
    ya_c = (h_c * jax.nn.gelu(gate_c.astype(jnp.float32))).astype(hc.dtype)
    qn_c, qr_c = mla_queries(cq_c, q_norm_w, w_uq)
    yb_c = blocked_attention(qn_c, qr_c, kn_c, kr_c, v_c)
    out_c = jnp.concatenate([ya_c, yb_c], axis=-1) @ w_out
    return out_c, out_x


def hgrn2_scan(q, k, logf, v, s0):
    B, T = k.shape[:2]
    nc = T // HG_CHUNK
    with_out = q is not None

    def to_chunks(t):
        return t.reshape(B, nc, HG_CHUNK, HG_HEADS, t.shape[-1]).transpose(1, 0, 3, 2, 4)

    causal = jnp.tril(jnp.ones((HG_CHUNK, HG_CHUNK), dtype=bool))[:, :, None]

    def step(s, blk):
        kc, gc, vc = blk[0], blk[1], blk[2]
        bcum = jnp.cumsum(gc, axis=2)
        b_last = bcum[:, :, -1:]
        s_new = (jnp.exp(b_last[:, :, 0])[..., None] * s
                 + jnp.einsum('bhsd,bhse->bhde', kc * jnp.exp(b_last - bcum), vc))
        if not with_out:
            return s_new, None
        qc = blk[3]
        o_inter = jnp.einsum('bhtd,bhde->bhte', qc * jnp.exp(bcum), s)
        diff = bcum[:, :, :, None, :] - bcum[:, :, None, :, :]
        decay = jnp.exp(jnp.where(causal, diff, -jnp.inf))
        attn = jnp.einsum('bhtd,bhsd,bhtsd->bhts', qc, kc, decay)
        o_intra = jnp.einsum('bhts,bhse->bhte', attn, vc)
        return s_new, o_inter + o_intra

    xs = (to_chunks(k), to_chunks(logf), to_chunks(v)) + ((to_chunks(q),) if with_out else ())
    s_fin, o = lax.scan(step, s0, xs)
    if with_out:
        o = o.transpose(1, 0, 3, 2, 4).reshape(B, T, HG_HEADS, HG_DV)
    return o, s_fin


def hgrn2_project(h, w_in, lb):
    B, T = h.shape[:2]
    p = (h @ w_in).astype(jnp.float32)
    q, f_fwd, f_bwd, i, g = jnp.split(p, [D_HG, 2 * D_HG, 3 * D_HG, 3 * D_HG + D_HG_V], axis=-1)

    def heads(t):
        return t.reshape(B, T, HG_HEADS, -1)

    dirs = []
    for f_raw in (f_fwd, f_bwd):
        f = lb + (1.0 - lb) * jax.nn.sigmoid(f_raw)
        dirs.append((heads(1.0 - f), heads(jnp.log(f))))
    return heads(q), dirs, heads(i), g


def mixer_hgrn2(hc, hx, w_in, lb, norm_w, w_out, need_ctx):
    qc, dc, vc, gc = hgrn2_project(hc, w_in, lb)
    qx, dx, vx, gx = hgrn2_project(hx, w_in, lb)
    B = hx.shape[0]
    s0 = jnp.zeros((B, HG_HEADS, HG_DK, HG_DV), jnp.float32)
    oc_dirs, ox_dirs = [], []
    for d in range(2):
        rev = _flip_seq if d == 1 else _same
        kc, lfc = dc[d]
        kx, lfx = dx[d]
        q_ctx = jax.nn.silu(rev(qc)) if need_ctx else None
        o_c, s_c = hgrn2_scan(q_ctx, rev(kc), rev(lfc), rev(vc), s0)
        o_x, _ = hgrn2_scan(jax.nn.silu(rev(qx)), rev(kx), rev(lfx), rev(vx), s_c)
        ox_dirs.append(rev(o_x))
        if need_ctx:
            oc_dirs.append(rev(o_c))

    def readout(o, g, dtype):
        Bn, T = o.shape[:2]
        o = rms_norm(o, norm_w) * jax.nn.silu(g).reshape(Bn, T, HG_HEADS, HG_DV)
        return o.reshape(Bn, T, D_HG_V).astype(dtype) @ w_out

    out_x = readout(ox_dirs[0] + ox_dirs[1], gx, hx.dtype)
    if not need_ctx:
        return None, out_x
    return readout(oc_dirs[0] + oc_dirs[1], gc, hc.dtype), out_x


def setup_inputs(seed: int = 0) -> dict:
    key = jax.random.key(seed)
    ks = list(jax.random.split(key, 32))

    def nrm(i, shape, scale):
        return scale * jax.random.normal(ks[i], shape, jnp.float32)

    def gain(i, shape):
        return 1.0 + nrm(i, shape, 0.05)

    u = jax.random.uniform(ks[16], (N_EVEN, 2, D_LRU), jnp.float32, 0.9, 0.999)
    a_base = u ** (1.0 / LRU_C)
    return {
        "x": nrm(0, (BATCH, SEQ, D_MODEL), 1.0),
        "c": nrm(1, (BATCH, D_MODEL), 1.0),
        "ctx": nrm(2, (BATCH, CTX_LEN, D_MODEL), 1.0),
        "c_ctx": nrm(3, (D_MODEL,), 1.0),
        "ada_w": nrm(4, (DEPTH, D_MODEL, N_MOD * D_MODEL), 0.5 * D_MODEL ** -0.5),
        "ada_b": nrm(5, (DEPTH, N_MOD * D_MODEL), 0.02),
        "norm_mix_w": gain(6, (DEPTH, D_MODEL)),
        "norm_mlp_w": gain(7, (DEPTH, D_MODEL)),
        "ab_w_in": nrm(8, (N_EVEN, D_MODEL, D_AB_IN), D_MODEL ** -0.5),
        "ab_w_out": nrm(9, (N_EVEN, D_AB_OUT, D_MODEL), D_AB_OUT ** -0.5),
        "lru_conv_w": nrm(10, (N_EVEN, CONV_W, D_LRU), CONV_W ** -0.5),
        "lru_conv_b": nrm(11, (N_EVEN, D_LRU), 0.02),
        "lru_w_a": nrm(12, (N_EVEN, 2, LRU_HEADS, LRU_HEAD_DIM, LRU_HEAD_DIM), LRU_HEAD_DIM ** -0.5),
        "lru_b_a": nrm(13, (N_EVEN, 2, D_LRU), 0.02),
        "lru_w_x": nrm(14, (N_EVEN, 2, LRU_HEADS, LRU_HEAD_DIM, LRU_HEAD_DIM), LRU_HEAD_DIM ** -0.5),
        "lru_b_x": nrm(15, (N_EVEN, 2, D_LRU), 0.02),
        "lru_lambda": jnp.log(a_base) - jnp.log1p(-a_base),
        "mla_q_norm_w": gain(17, (N_EVEN, Q_LORA)),
        "mla_w_uq": nrm(18, (N_EVEN, Q_LORA, MLA_HEADS * (QK_NOPE + QK_ROPE)), Q_LORA ** -0.5),
        "mla_kv_norm_w": gain(19, (N_EVEN, KV_LORA)),
        "mla_w_ukv": nrm(20, (N_EVEN, KV_LORA, MLA_HEADS * (QK_NOPE + V_HEAD)), KV_LORA ** -0.5),
        "hg_w_in": nrm(21, (N_ODD, D_MODEL, D_HG_IN), D_MODEL ** -0.5),
        "hg_lb_logits": nrm(22, (DEPTH, D_HG), 0.1),
        "hg_norm_w": gain(23, (N_ODD, HG_DV)),
        "hg_w_out": nrm(24, (N_ODD, D_HG_V, D_MODEL), D_HG_V ** -0.5),
        "mlp_w1": nrm(25, (DEPTH, D_MODEL, D_FF), D_MODEL ** -0.5),
        "mlp_w2": nrm(26, (DEPTH, D_FF, D_MODEL), D_FF ** -0.5),
        "final_norm_w": gain(27, (D_MODEL,)),
    }


def reference(x, c, ctx, c_ctx, ada_w, ada_b, norm_mix_w, norm_mlp_w, ab_w_in, ab_w_out,
              lru_conv_w, lru_conv_b, lru_w_a, lru_b_a, lru_w_x, lru_b_x, lru_lambda,
              mla_q_norm_w, mla_w_uq, mla_kv_norm_w, mla_w_ukv,
              hg_w_in, hg_lb_logits, hg_norm_w, hg_w_out, mlp_w1, mlp_w2, final_norm_w):
    n_lat = x.shape[1]
    rows_n = n_lat // GRID_W
    row = jnp.repeat(jnp.arange(rows_n), GRID_W)
    col = jnp.tile(jnp.arange(GRID_W), rows_n)
    cos, sin = axial_rope_tables(row, col)
    p_lb = jax.nn.softmax(hg_lb_logits.astype(jnp.float32), axis=0)
    lb_all = jnp.cumsum(p_lb, axis=0) - p_lb[0]
    for l in range(DEPTH):
        last = l == DEPTH - 1
        sh1, sc1, g1, sh2, sc2, g2 = ada_params(c, ada_w[l], ada_b[l])
        csh1, csc1, cg1, csh2, csc2, cg2 = ada_params(c_ctx, ada_w[l], ada_b[l])
        hx = modulate(rms_norm(x, norm_mix_w[l]), sh1, sc1)
        hc = modulate(rms_norm(ctx, norm_mix_w[l]), csh1, csc1)
        if l % 2 == 0:
            e = l // 2
            mc, mx = mixer_ab(hc, hx, cos, sin, ab_w_in[e], ab_w_out[e], lru_conv_w[e], lru_conv_b[e],
                              lru_w_a[e], lru_b_a[e], lru_w_x[e], lru_b_x[e], lru_lambda[e],
                              mla_q_norm_w[e], mla_w_uq[e], mla_kv_norm_w[e], mla_w_ukv[e], not last)
        else:
            o = l // 2
            mc, mx = mixer_hgrn2(hc, hx, hg_w_in[o], lb_all[l], hg_norm_w[o], hg_w_out[o], not last)
        x = x + g1 * mx
        x = x + g2 * squared_relu_mlp(modulate(rms_norm(x, norm_mlp_w[l]), sh2, sc2), mlp_w1[l], mlp_w2[l])
        if not last:
            ctx = ctx + cg1 * mc
            ctx = ctx + cg2 * squared_relu_mlp(modulate(rms_norm(ctx, norm_mlp_w[l]), csh2, csc2),
                                               mlp_w1[l], mlp_w2[l])
    return rms_norm(x, final_norm_w)
```

```python
import functools

import numpy as np
import jax
import jax.numpy as jnp
from jax import lax
from jax.experimental import pallas as pl
from jax.experimental.pallas import tpu as pltpu

GRID_W = 64
EPS = 1e-6
N_MOD = 6
LRU_HEADS = 8
CONV_W = 4
CONV_LEFT = CONV_W // 2
LRU_C = 8.0
MLA_HEADS = 8
QK_NOPE = 128
QK_ROPE = 64
V_HEAD = 128
ROPE_BASE = 10000.0
HG_DK = 128
HG_CHUNK = 64

LANES = 128
SUBLANES = 8
VMEM_LIMIT_BYTES = 56 * 1024 * 1024

F32 = jnp.float32
BF16 = jnp.bfloat16

HG_LEVELS = int(np.log2(HG_CHUNK))
HG_EXP_ROWS = (2 + HG_LEVELS) * HG_CHUNK


def _cparams(*sem):
    return pltpu.CompilerParams(dimension_semantics=sem, vmem_limit_bytes=VMEM_LIMIT_BYTES)


def _dot(a, b):
    return jnp.dot(a, b, preferred_element_type=F32)


def _dot_nt(a, b):
    return lax.dot_general(a, b, (((1,), (1,)), ((), ())), preferred_element_type=F32)


def _dot_tn(a, b):
    return lax.dot_general(a, b, (((0,), (0,)), ((), ())), preferred_element_type=F32)


def _sigmoid(x):
    return 1.0 / (1.0 + jnp.exp(-x))


def _silu(x):
    return x * _sigmoid(x)


def _gelu_tanh(x):
    c = np.float32(np.sqrt(2.0 / np.pi))
    return 0.5 * x * (1.0 + jnp.tanh(c * (x + np.float32(0.044715) * (x * x * x))))


def _softplus(x):
    return jnp.maximum(x, 0.0) + jnp.log1p(jnp.exp(-jnp.abs(x)))


def _row_tile(t_total):
    tm = t_total // 8
    assert tm * 8 == t_total and tm % 16 == 0, t_total
    return tm


def _ada_kernel(c_ref, w_ref, b_ref, o_ref):
    a = _silu(c_ref[...]).astype(BF16)
    o_ref[...] = _dot(a, w_ref[...].astype(BF16)) + b_ref[...]


def _ada_all(cond8, ada_w, ada_b):
    n_layers, d, n6 = ada_w.shape
    tn = 1024 if n6 % 1024 == 0 else n6 // N_MOD
    return pl.pallas_call(
        _ada_kernel,
        out_shape=jax.ShapeDtypeStruct((n_layers, SUBLANES, n6), F32),
        grid=(n_layers, n6 // tn),
        in_specs=[
            pl.BlockSpec((SUBLANES, d), lambda l, j: (0, 0)),
            pl.BlockSpec((None, d, tn), lambda l, j: (l, 0, j)),
            pl.BlockSpec((None, 1, tn), lambda l, j: (l, 0, j)),
        ],
        out_specs=pl.BlockSpec((None, SUBLANES, tn), lambda l, j: (l, 0, j)),
        compiler_params=_cparams("arbitrary", "arbitrary"),
        name="ada_params",
    )(cond8, ada_w, ada_b.reshape(n_layers, 1, n6))


def _lb_kernel(x_ref, o_ref):
    n = x_ref.shape[0]
    rows = [x_ref[pl.ds(l, 1), :] for l in range(n)]
    mx = functools.reduce(jnp.maximum, rows)
    es = [jnp.exp(r - mx) for r in rows]
    tot = functools.reduce(lambda a, b: a + b, es)
    ps = [e / tot for e in es]
    acc = ps[0]
    o_ref[pl.ds(0, 1), :] = acc - ps[0]
    for l in range(1, n):
        acc = acc + ps[l]
        o_ref[pl.ds(l, 1), :] = acc - ps[0]


def _lower_bounds(logits):
    return pl.pallas_call(
        _lb_kernel, out_shape=jax.ShapeDtypeStruct(logits.shape, F32), name="hgrn2_lower_bounds"
    )(logits.astype(F32))


def _mod_rows(ref, b, n_batch, is_ctx):
    own = ref[pl.ds(b, 1), :]
    ctx = ref[pl.ds(n_batch, 1), :]
    return jnp.where(is_ctx, ctx, own)


def _is_ctx_rows(tm, seq):
    i = pl.program_id(1)
    row = i * tm + lax.broadcasted_iota(jnp.int32, (tm, 1), 0)
    return row >= seq


def _norm_modulate(x, nw, sh, sc):
    ms = jnp.mean(x * x, axis=-1, keepdims=True)
    y = x * lax.rsqrt(ms + EPS) * nw
    return y * (1.0 + sc) + sh


def _inproj_kernel(x_ref, nw_ref, sh_ref, sc_ref, w_ref, o_ref, h_ref, *, n_batch, seq):
    b = pl.program_id(0)
    tm = x_ref.shape[0]

    @pl.when(pl.program_id(2) == 0)
    def _():
        is_ctx = _is_ctx_rows(tm, seq)
        sh = _mod_rows(sh_ref, b, n_batch, is_ctx)
        sc = _mod_rows(sc_ref, b, n_batch, is_ctx)
        h_ref[...] = _norm_modulate(x_ref[...], nw_ref[...], sh, sc).astype(BF16)

    o_ref[...] = _dot(h_ref[...], w_ref[...]).astype(o_ref.dtype)


def _in_proj(xs, norm_w, mods, w, *, seq, tn, shift_idx=0, scale_idx=1):
    n_batch, t, d = xs.shape
    n_out = w.shape[1]
    tm = _row_tile(t)
    assert n_out % tn == 0
    kern = functools.partial(_inproj_kernel, n_batch=n_batch, seq=seq)
    return pl.pallas_call(
        kern,
        out_shape=jax.ShapeDtypeStruct((n_batch, t, n_out), F32),
        grid=(n_batch, t // tm, n_out // tn),
        in_specs=[
            pl.BlockSpec((None, tm, d), lambda b, i, j: (b, i, 0)),
            pl.BlockSpec((1, d), lambda b, i, j: (0, 0)),
            pl.BlockSpec((SUBLANES, d), lambda b, i, j: (0, shift_idx)),
            pl.BlockSpec((SUBLANES, d), lambda b, i, j: (0, scale_idx)),
            pl.BlockSpec((d, tn), lambda b, i, j: (0, j)),
        ],
        out_specs=pl.BlockSpec((None, tm, tn), lambda b, i, j: (b, i, j)),
        scratch_shapes=[pltpu.VMEM((tm, d), BF16)],
        compiler_params=_cparams("arbitrary", "arbitrary", "arbitrary"),
        name="in_proj",
    )(xs, norm_w.reshape(1, d), mods, mods, w)


def _mlp_kernel(x_ref, nw_ref, sh_ref, sc_ref, g_ref, w1_ref, w2_ref, o_ref, h_ref, acc_ref,
                *, n_batch, seq):
    b = pl.program_id(0)
    f = pl.program_id(2)
    tm = x_ref.shape[0]

    @pl.when(f == 0)
    def _():
        is_ctx = _is_ctx_rows(tm, seq)
        sh = _mod_rows(sh_ref, b, n_batch, is_ctx)
        sc = _mod_rows(sc_ref, b, n_batch, is_ctx)
        h_ref[...] = _norm_modulate(x_ref[...], nw_ref[...], sh, sc).astype(BF16)
        acc_ref[...] = jnp.zeros_like(acc_ref)

    a = jnp.maximum(_dot(h_ref[...], w1_ref[...]), 0.0)
    acc_ref[...] += _dot((a * a).astype(BF16), w2_ref[...])

    @pl.when(f == pl.num_programs(2) - 1)
    def _():
        is_ctx = _is_ctx_rows(tm, seq)
        g = _mod_rows(g_ref, b, n_batch, is_ctx)
        o_ref[...] = x_ref[...] + g * acc_ref[...]


def _mlp(xs, norm_w, mods, w1, w2, *, seq):
    n_batch, t, d = xs.shape
    d_ff = w1.shape[1]
    tm = _row_tile(t)
    tf = 1024 if d_ff % 1024 == 0 else d_ff
    kern = functools.partial(_mlp_kernel, n_batch=n_batch, seq=seq)
    return pl.pallas_call(
        kern,
        out_shape=jax.ShapeDtypeStruct((n_batch, t, d), F32),
        grid=(n_batch, t // tm, d_ff // tf),
        in_specs=[
            pl.BlockSpec((None, tm, d), lambda b, i, f: (b, i, 0)),
            pl.BlockSpec((1, d), lambda b, i, f: (0, 0)),
            pl.BlockSpec((SUBLANES, d), lambda b, i, f: (0, 3)),
            pl.BlockSpec((SUBLANES, d), lambda b, i, f: (0, 4)),
            pl.BlockSpec((SUBLANES, d), lambda b, i, f: (0, 5)),
            pl.BlockSpec((d, tf), lambda b, i, f: (0, f)),
            pl.BlockSpec((tf, d), lambda b, i, f: (f, 0)),
        ],
        out_specs=pl.BlockSpec((None, tm, d), lambda b, i, f: (b, i, 0)),
        scratch_shapes=[pltpu.VMEM((tm, d), BF16), pltpu.VMEM((tm, d), F32)],
        compiler_params=_cparams("arbitrary", "arbitrary", "arbitrary"),
        name="mlp",
    )(xs, norm_w.reshape(1, d), mods, mods, mods, w1, w2)


def _outproj_ab_kernel(ya_ref, yb_ref, w_ref, x_ref, g_ref, o_ref, *, n_batch, seq):
    b = pl.program_id(0)
    tm = x_ref.shape[0]
    da = ya_ref.shape[1]
    y = _dot(ya_ref[...], w_ref[0:da, :]) + _dot(yb_ref[...], w_ref[da:, :])
    g = _mod_rows(g_ref, b, n_batch, _is_ctx_rows(tm, seq))
    o_ref[...] = x_ref[...] + g * y


def _out_proj_ab(ya, yb, w_out, xs, mods, *, seq):
    n_batch, t, d = xs.shape
    da, db = ya.shape[2], yb.shape[2]
    tm = _row_tile(t)
    kern = functools.partial(_outproj_ab_kernel, n_batch=n_batch, seq=seq)
    return pl.pallas_call(
        kern,
        out_shape=jax.ShapeDtypeStruct((n_batch, t, d), F32),
        grid=(n_batch, t // tm),
        in_specs=[
            pl.BlockSpec((None, tm, da), lambda b, i: (b, i, 0)),
            pl.BlockSpec((None, tm, db), lambda b, i: (b, i, 0)),
            pl.BlockSpec((da + db, d), lambda b, i: (0, 0)),
            pl.BlockSpec((None, tm, d), lambda b, i: (b, i, 0)),
            pl.BlockSpec((SUBLANES, d), lambda b, i: (0, 2)),
        ],
        out_specs=pl.BlockSpec((None, tm, d), lambda b, i: (b, i, 0)),
        compiler_params=_cparams("arbitrary", "arbitrary"),
        name="out_proj_ab",
    )(ya, yb, w_out, xs, mods)


def _lru_kernel(gate_ref, u_ref, cw_ref, cb_ref, wa_ref, ba_ref, wx_ref, bx_ref, lam_ref, o_ref,
                upad_ref, af_ref, bf_ref, ab_ref, bb_ref, *, seq, chunk):
    t_total = u_ref.shape[0]
    n_ctx = t_total - seq
    pad = SUBLANES
    cw = cw_ref[...]
    cb = cb_ref[...]
    sp = _softplus(-lam_ref[...])
    zeros_pad = jnp.zeros((pad, LANES), F32)

    def gates(uc, d):
        ucb = uc.astype(BF16)
        r = _sigmoid(_dot(ucb, wa_ref[d]) + ba_ref[pl.ds(d, 1), :])
        i = _sigmoid(_dot(ucb, wx_ref[d]) + bx_ref[pl.ds(d, 1), :])
        log_a = (-LRU_C) * r * sp[d:d + 1, :]
        a = jnp.exp(log_a)
        return a, jnp.sqrt(1.0 - a * a) * (i * uc)

    def prepare(src0, dst0, n):
        upad_ref[pl.ds(0, pad), :] = zeros_pad
        upad_ref[pl.ds(pad + n, pad), :] = zeros_pad
        upad_ref[pl.ds(pad, n), :] = u_ref[pl.ds(src0, n), :]
        ch = min(chunk, n)

        def body(c, carry):
            r0 = pl.multiple_of(c * ch, SUBLANES)
            win = upad_ref[pl.ds(r0, ch + 2 * pad), :]
            uc = cb
            for j in range(CONV_W):
                off = pad + j - CONV_LEFT
                uc = uc + win[off:off + ch, :] * cw[j:j + 1, :]
            a0, b0 = gates(uc, 0)
            a1, b1 = gates(uc, 1)
            d0 = pl.multiple_of(dst0 + r0, SUBLANES)
            af_ref[pl.ds(d0, ch), :] = a0
            bf_ref[pl.ds(d0, ch), :] = b0
            ab_ref[pl.ds(d0, ch), :] = a1
            bb_ref[pl.ds(d0, ch), :] = b1
            return carry

        lax.fori_loop(0, n // ch, body, 0)

    prepare(seq, 0, n_ctx)
    prepare(0, n_ctx, seq)

    row = lax.broadcasted_iota(jnp.int32, (SUBLANES, LANES), 0)

    def scan_fwd(a, b):
        for s in (1, 2, 4):
            m = row >= s
            a_s = pltpu.roll(a, s, 0)
            b_s = pltpu.roll(b, s, 0)
            b = jnp.where(m, a * b_s + b, b)
            a = jnp.where(m, a * a_s, a)
        return a, b

    def scan_bwd(a, b):
        for s in (1, 2, 4):
            m = row < SUBLANES - s
            a_s = pltpu.roll(a, SUBLANES - s, 0)
            b_s = pltpu.roll(b, SUBLANES - s, 0)
            b = jnp.where(m, a * b_s + b, b)
            a = jnp.where(m, a * a_s, a)
        return a, b

    def scan_range(g0, n_groups, carry):
        def body(g, c):
            cf, cbk = c
            rf = pl.multiple_of((g0 + g) * SUBLANES, SUBLANES)
            a, b = scan_fwd(af_ref[pl.ds(rf, SUBLANES), :], bf_ref[pl.ds(rf, SUBLANES), :])
            hf = a * cf + b
            bf_ref[pl.ds(rf, SUBLANES), :] = hf
            rb = pl.multiple_of((g0 + n_groups - 1 - g) * SUBLANES, SUBLANES)
            a, b = scan_bwd(ab_ref[pl.ds(rb, SUBLANES), :], bb_ref[pl.ds(rb, SUBLANES), :])
            hb = a * cbk + b
            bb_ref[pl.ds(rb, SUBLANES), :] = hb
            cf = jnp.broadcast_to(hf[SUBLANES - 1:SUBLANES, :], (SUBLANES, LANES))
            cbk = jnp.broadcast_to(hb[0:1, :], (SUBLANES, LANES))
            return cf, cbk

        return lax.fori_loop(0, n_groups, body, carry, unroll=4)

    zero = jnp.zeros((SUBLANES, LANES), F32)
    carry = scan_range(0, n_ctx // SUBLANES, (zero, zero))
    scan_range(n_ctx // SUBLANES, seq // SUBLANES, carry)

    def finish(src0, dst0, n):
        ch = min(chunk, n)

        def body(c, carry):
            r0 = c * ch
            s0 = pl.multiple_of(src0 + r0, SUBLANES)
            d0 = pl.multiple_of(dst0 + r0, SUBLANES)
            h = bf_ref[pl.ds(s0, ch), :] + bb_ref[pl.ds(s0, ch), :]
            o_ref[pl.ds(d0, ch), :] = (h * _gelu_tanh(gate_ref[pl.ds(d0, ch), :])).astype(o_ref.dtype)
            return carry

        lax.fori_loop(0, n // ch, body, 0)

    finish(0, seq, n_ctx)
    finish(n_ctx, 0, seq)


def _lru(p, conv_w, conv_b, w_a, b_a, w_x, b_x, lam, *, seq):
    n_batch, t, _ = p.shape
    d_lru = conv_w.shape[1]
    n_heads = w_a.shape[1]
    assert d_lru // n_heads == LANES
    n_ctx = t - seq
    chunk = 256
    assert seq % min(chunk, seq) == 0 and n_ctx % min(chunk, n_ctx) == 0
    kern = functools.partial(_lru_kernel, seq=seq, chunk=chunk)
    vec = lambda rows: pl.BlockSpec((rows, LANES), lambda b, h: (0, h))
    wspec = pl.BlockSpec((2, None, LANES, LANES), lambda b, h: (0, h, 0, 0))
    scratch_rows = pltpu.VMEM((t, LANES), F32)
    return pl.pallas_call(
        kern,
        out_shape=jax.ShapeDtypeStruct((n_batch, t, d_lru), BF16),
        grid=(n_batch, n_heads),
        in_specs=[
            pl.BlockSpec((None, t, LANES), lambda b, h: (b, 0, h)),
            pl.BlockSpec((None, t, LANES), lambda b, h: (b, 0, n_heads + h)),
            vec(CONV_W), vec(1), wspec, vec(2), wspec, vec(2), vec(2),
        ],
        out_specs=pl.BlockSpec((None, t, LANES), lambda b, h: (b, 0, h)),
        scratch_shapes=[pltpu.VMEM((max(seq, n_ctx) + 2 * SUBLANES, LANES), F32),
                        scratch_rows, scratch_rows, scratch_rows, scratch_rows],
        compiler_params=_cparams("arbitrary", "arbitrary"),
        name="rglru",
    )(p, p, conv_w, conv_b.reshape(1, d_lru), w_a, b_a, w_x, b_x, lam)


def _mla_qkv_kernel(cq_ref, ckv_ref, kr_ref, qn_ref, kvn_ref, wq_ref, wkv_ref, cos_ref, sin_ref,
                    q_ref, k_ref, v_ref, *, scale):
    def rms(x, w):
        return (x * lax.rsqrt(jnp.mean(x * x, axis=-1, keepdims=True) + EPS) * w).astype(BF16)

    cos = cos_ref[...]
    sin = sin_ref[...]
    q_all = _dot(rms(cq_ref[...], qn_ref[...]), wq_ref[...])
    kv_all = _dot(rms(ckv_ref[...], kvn_ref[...]), wkv_ref[...])
    kr = kr_ref[...]
    kr_rot = kr[:, :QK_ROPE] * cos + kr[:, QK_ROPE:] * sin
    wq_head = QK_NOPE + 2 * QK_ROPE
    wkv_head = QK_NOPE + V_HEAD
    for h in range(q_ref.shape[0]):
        qh = q_all[:, h * wq_head:(h + 1) * wq_head]
        qr = qh[:, QK_NOPE:QK_NOPE + QK_ROPE] * cos + qh[:, QK_NOPE + QK_ROPE:] * sin
        q_ref[h, :, 0:QK_NOPE] = (qh[:, :QK_NOPE] * scale).astype(BF16)
        q_ref[h, :, QK_NOPE:] = (qr * scale).astype(BF16)
        kvh = kv_all[:, h * wkv_head:(h + 1) * wkv_head]
        k_ref[h, :, 0:QK_NOPE] = kvh[:, :QK_NOPE].astype(BF16)
        k_ref[h, :, QK_NOPE:] = kr_rot.astype(BF16)
        v_ref[h] = kvh[:, QK_NOPE:].astype(BF16)


def _mla_qkv(p, q_norm_w, kv_norm_w, wq_ext, wkv, cos_t, sin_t, *, seq, col0):
    n_batch, t, _ = p.shape
    q_lora = q_norm_w.shape[0]
    kv_lora = kv_norm_w.shape[0]
    assert q_lora == kv_lora and col0 % q_lora == 0 and (col0 + 2 * q_lora) % LANES == 0
    n_ctx = t - seq
    tm = n_ctx
    nh = MLA_HEADS
    dqk = QK_NOPE + QK_ROPE
    scale = np.float32(dqk ** -0.5)
    cq_blk = col0 // q_lora
    kr_blk = (col0 + 2 * q_lora) // LANES
    full = lambda shape: pl.BlockSpec(shape, lambda b, i: (0,) * len(shape))
    return pl.pallas_call(
        functools.partial(_mla_qkv_kernel, scale=scale),
        out_shape=(jax.ShapeDtypeStruct((n_batch, nh, t, dqk), BF16),
                   jax.ShapeDtypeStruct((n_batch, nh, t, dqk), BF16),
                   jax.ShapeDtypeStruct((n_batch, nh, t, V_HEAD), BF16)),
        grid=(n_batch, t // tm),
        in_specs=[
            pl.BlockSpec((None, tm, q_lora), lambda b, i: (b, i, cq_blk)),
            pl.BlockSpec((None, tm, kv_lora), lambda b, i: (b, i, cq_blk + 1)),
            pl.BlockSpec((None, tm, LANES), lambda b, i: (b, i, kr_blk)),
            full((1, q_lora)), full((1, kv_lora)), full(wq_ext.shape), full(wkv.shape),
            pl.BlockSpec((tm, QK_ROPE), lambda b, i: (i, 0)),
            pl.BlockSpec((tm, QK_ROPE), lambda b, i: (i, 0)),
        ],
        out_specs=(pl.BlockSpec((None, nh, tm, dqk), lambda b, i: (b, 0, i, 0)),
                   pl.BlockSpec((None, nh, tm, dqk), lambda b, i: (b, 0, i, 0)),
                   pl.BlockSpec((None, nh, tm, V_HEAD), lambda b, i: (b, 0, i, 0))),
        compiler_params=_cparams("arbitrary", "arbitrary"),
        name="mla_qkv",
    )(p, p, p, q_norm_w.reshape(1, q_lora), kv_norm_w.reshape(1, kv_lora), wq_ext, wkv, cos_t, sin_t)


def _attn_kernel(q_ref, k_ref, v_ref, o_ref, *, n_kv, tk):
    i = pl.program_id(2)
    q = q_ref[...]
    tq = q.shape[0]

    def step(j, carry):
        m, l, acc = carry
        r0 = pl.multiple_of(j * tk, tk)
        s = _dot_nt(q, k_ref[pl.ds(r0, tk), :])
        m_new = jnp.maximum(m, jnp.max(s, axis=-1, keepdims=True))
        p = jnp.exp(s - m_new)
        alpha = jnp.exp(m - m_new)
        l = alpha * l + jnp.sum(p, axis=-1, keepdims=True)
        acc = alpha * acc + _dot(p.astype(BF16), v_ref[pl.ds(r0, tk), :])
        return m_new, l, acc

    init = (jnp.full((tq, 1), -jnp.inf, F32), jnp.zeros((tq, 1), F32), jnp.zeros((tq, V_HEAD), F32))
    is_ctx_tile = i == n_kv - 1

    @pl.when(is_ctx_tile)
    def _():
        m, l, acc = step(n_kv - 1, init)
        o_ref[...] = (acc / l).astype(o_ref.dtype)

    @pl.when(jnp.logical_not(is_ctx_tile))
    def _():
        m, l, acc = lax.fori_loop(0, n_kv, step, init)
        o_ref[...] = (acc / l).astype(o_ref.dtype)


def _attention(q, k, v, *, seq):
    n_batch, nh, t, dqk = q.shape
    tq = t - seq
    assert seq % tq == 0
    n_kv = t // tq
    return pl.pallas_call(
        functools.partial(_attn_kernel, n_kv=n_kv, tk=tq),
        out_shape=jax.ShapeDtypeStruct((n_batch, t, nh * V_HEAD), BF16),
        grid=(n_batch, nh, t // tq),
        in_specs=[
            pl.BlockSpec((None, None, tq, dqk), lambda b, h, i: (b, h, i, 0)),
            pl.BlockSpec((None, None, t, dqk), lambda b, h, i: (b, h, 0, 0)),
            pl.BlockSpec((None, None, t, V_HEAD), lambda b, h, i: (b, h, 0, 0)),
        ],
        out_specs=pl.BlockSpec((None, tq, V_HEAD), lambda b, h, i: (b, i, h)),
        compiler_params=_cparams("arbitrary", "arbitrary", "arbitrary"),
        name="mla_attention",
    )(q, k, v)


def _hgrn2_tables():
    n = HG_CHUNK
    t = np.arange(n)[:, None]
    r = np.arange(n)[None, :]
    fwd = [(r <= t), (r > t)]
    bwd = [(r >= t), (r < t)]
    for lvl in range(HG_LEVELS):
        m = 1 << lvl
        e = (t >> (lvl + 1)) * 2 * m + m - 1
        right = ((t >> lvl) & 1) == 1
        fwd.append(np.where(right, (r > e) & (r <= t), (r > t) & (r <= e)))
        bwd.append(np.where(right, (r > e) & (r < t), (r >= t) & (r <= e)))
    x = t ^ r
    level = np.floor(np.log2(np.maximum(x, 1))).astype(np.int32)
    lv_f = np.where(r < t, level, np.where(r == t, HG_LEVELS, HG_LEVELS + 1)).astype(np.int32)
    to = lambda ms: np.concatenate(ms, axis=0).astype(np.float32)
    return to(fwd), to(bwd), lv_f, np.ascontiguousarray(lv_f.T)


def _hgrn2_kernel(qf_ref, ff_ref, vf_ref, qb_ref, fb_ref, vb_ref, lb_ref, mf_ref, mb_ref, lvf_ref,
                  lvb_ref, of_ref, ob_ref, sf_ref, sb_ref, *, heads):
    n = HG_CHUNK
    n_chunks = qf_ref.shape[0] // n

    @pl.when(pl.program_id(2) == 0)
    def _():
        sf_ref[...] = jnp.zeros_like(sf_ref)
        sb_ref[...] = jnp.zeros_like(sb_ref)

    lb = lb_ref[...]
    row = lax.broadcasted_iota(jnp.int32, (n, 1), 0)

    def one_chunk(q_ref, f_ref, v_ref, m_ref, lv_ref, o_ref, s_ref, r0, backward):
        f = lb + (1.0 - lb) * _sigmoid(f_ref[pl.ds(r0, n), :])
        g = jnp.log(f)
        g_hi = g.astype(BF16)
        g_lo = (g - g_hi.astype(F32)).astype(BF16)
        m_all = m_ref[...]
        x_all = jnp.exp(_dot(m_all, g_hi) + _dot(m_all, g_lo))
        q_all = _silu(q_ref[pl.ds(r0, n), :])
        k_all = 1.0 - f
        v_all = v_ref[pl.ds(r0, n), :].astype(BF16)
        lv = lv_ref[...]
        outs = []
        for h in range(heads):
            cs = slice(h * HG_DK, (h + 1) * HG_DK)
            q, k, v = q_all[:, cs], k_all[:, cs], v_all[:, cs]
            xs = x_all[:, cs]
            eq, ek = xs[0:n], xs[n:2 * n]
            tot = eq[0:1] if backward else eq[n - 1:n]
            a = jnp.zeros((n, n), F32)
            for lvl in range(HG_LEVELS):
                is_q = ((row >> lvl) & 1) == (0 if backward else 1)
                z = (xs[(2 + lvl) * n:(3 + lvl) * n] * jnp.where(is_q, q, k)).astype(BF16)
                a = jnp.where(lv == lvl, _dot_nt(z, z), a)
            a = jnp.where(lv == HG_LEVELS, _dot_nt(q.astype(BF16), k.astype(BF16)), a)
            s_t = s_ref[h]
            o = _dot_nt((q * eq).astype(BF16), s_t.astype(BF16)) + _dot(a.astype(BF16), v)
            s_ref[h] = s_t * tot + _dot_tn(v, (k * ek).astype(BF16))
            outs.append(o)
        o_ref[pl.ds(r0, n), :] = outs[0] if heads == 1 else jnp.concatenate(outs, axis=1)

    def body(c, carry):
        rf = pl.multiple_of(c * n, n)
        one_chunk(qf_ref, ff_ref, vf_ref, mf_ref, lvf_ref, of_ref, sf_ref, rf, False)
        rb = pl.multiple_of((n_chunks - 1 - c) * n, n)
        one_chunk(qb_ref, fb_ref, vb_ref, mb_ref, lvb_ref, ob_ref, sb_ref, rb, True)
        return carry

    lax.fori_loop(0, n_chunks, body, 0)


def _hgrn2(p, lb, *, seq, heads_per_step=2):
    n_batch, t, n_in = p.shape
    d_k = lb.shape[-1]
    n_heads = d_k // HG_DK
    hb = heads_per_step if n_heads % heads_per_step == 0 else 1
    wb = hb * HG_DK
    tt = t - seq
    assert seq % tt == 0 and tt % HG_CHUNK == 0 and n_in == 5 * d_k
    nt = t // tt
    ncol = d_k // wb
    mf, mb, lvf, lvb = _hgrn2_tables()

    def tile_f(s):
        return jnp.where(s == 0, nt - 1, s - 1)

    def tile_b(s):
        return jnp.where(s == 0, nt - 1, nt - 1 - s)

    def col(tile, grp):
        return pl.BlockSpec((None, tt, wb), lambda b, g, s: (b, tile(s), grp * ncol + g))

    const = lambda shape: pl.BlockSpec(shape, lambda b, g, s: (0, 0))
    out_sd = jax.ShapeDtypeStruct((n_batch, t, d_k), F32)
    return pl.pallas_call(
        functools.partial(_hgrn2_kernel, heads=hb),
        out_shape=(out_sd, out_sd),
        grid=(n_batch, ncol, nt),
        in_specs=[col(tile_f, 0), col(tile_f, 1), col(tile_f, 3),
                  col(tile_b, 0), col(tile_b, 2), col(tile_b, 3),
                  pl.BlockSpec((1, wb), lambda b, g, s: (0, g)),
                  const(mf.shape), const(mb.shape), const(lvf.shape), const(lvb.shape)],
        out_specs=(pl.BlockSpec((None, tt, wb), lambda b, g, s: (b, tile_f(s), g)),
                   pl.BlockSpec((None, tt, wb), lambda b, g, s: (b, tile_b(s), g))),
        scratch_shapes=[pltpu.VMEM((hb, HG_DK, HG_DK), F32), pltpu.VMEM((hb, HG_DK, HG_DK), F32)],
        compiler_params=_cparams("arbitrary", "arbitrary", "arbitrary"),
        name="hgrn2_scan",
    )(p, p, p, p, p, p, lb.reshape(1, d_k), jnp.asarray(mf, BF16), jnp.asarray(mb, BF16),
      jnp.asarray(lvf), jnp.asarray(lvb))


def _outproj_hg_kernel(of_ref, ob_ref, gate_ref, nw_ref, w_ref, x_ref, g_ref, o_ref, y_ref,
                       *, n_batch, seq):
    b = pl.program_id(0)
    tm = x_ref.shape[0]
    nw = nw_ref[...]
    dv = nw.shape[1]
    for h in range(of_ref.shape[1] // dv):
        cs = slice(h * dv, (h + 1) * dv)
        o = of_ref[:, cs] + ob_ref[:, cs]
        y = o * lax.rsqrt(jnp.mean(o * o, axis=-1, keepdims=True) + EPS) * nw
        y_ref[:, cs] = (y * _silu(gate_ref[:, cs])).astype(BF16)
    g = _mod_rows(g_ref, b, n_batch, _is_ctx_rows(tm, seq))
    o_ref[...] = x_ref[...] + g * _dot(y_ref[...], w_ref[...])


def _out_proj_hg(o_f, o_b, p, norm_w, w_out, xs, mods, *, seq):
    n_batch, t, d = xs.shape
    d_v = o_f.shape[2]
    tm = _row_tile(t) // 2
    assert tm % 16 == 0
    gate_blk = (p.shape[2] - d_v) // d_v
    kern = functools.partial(_outproj_hg_kernel, n_batch=n_batch, seq=seq)
    rows = lambda width, cblk: pl.BlockSpec((None, tm, width), lambda b, i: (b, i, cblk))
    return pl.pallas_call(
        kern,
        out_shape=jax.ShapeDtypeStruct((n_batch, t, d), F32),
        grid=(n_batch, t // tm),
        in_specs=[
            rows(d_v, 0), rows(d_v, 0), rows(d_v, gate_blk),
            pl.BlockSpec((1, norm_w.shape[0]), lambda b, i: (0, 0)),
            pl.BlockSpec((d_v, d), lambda b, i: (0, 0)),
            rows(d, 0),
            pl.BlockSpec((SUBLANES, d), lambda b, i: (0, 2)),
        ],
        out_specs=rows(d, 0),
        scratch_shapes=[pltpu.VMEM((tm, d_v), BF16)],
        compiler_params=_cparams("arbitrary", "arbitrary"),
        name="out_proj_hgrn2",
    )(o_f, o_b, p, norm_w.reshape(1, -1), w_out, xs, mods)


def _final_norm_kernel(x_ref, w_ref, o_ref):
    x = x_ref[...]
    o_ref[...] = x * lax.rsqrt(jnp.mean(x * x, axis=-1, keepdims=True) + EPS) * w_ref[...]


def _final_norm(xs, w, *, seq):
    n_batch, t, d = xs.shape
    tm = t - seq
    return pl.pallas_call(
        _final_norm_kernel,
        out_shape=jax.ShapeDtypeStruct((n_batch, seq, d), F32),
        grid=(n_batch, seq // tm),
        in_specs=[pl.BlockSpec((None, tm, d), lambda b, i: (b, i, 0)),
                  pl.BlockSpec((1, d), lambda b, i: (0, 0))],
        out_specs=pl.BlockSpec((None, tm, d), lambda b, i: (b, i, 0)),
        compiler_params=_cparams("arbitrary", "arbitrary"),
        name="final_norm",
    )(xs, w.reshape(1, d))


def _rot_columns(w):
    q = QK_ROPE // 4
    return jnp.concatenate([-w[..., q:2 * q], w[..., :q], -w[..., 3 * q:], w[..., 2 * q:3 * q]], axis=-1)


def _rope_tables(seq, n_ctx):
    t = np.arange(seq)
    row, colm = t // GRID_W, t % GRID_W
    half = QK_ROPE // 2
    inv_freq = ROPE_BASE ** (-np.arange(0, half, 2, dtype=np.float32) / half)
    ang_r = row.astype(np.float32)[:, None] * inv_freq
    ang_c = colm.astype(np.float32)[:, None] * inv_freq
    ang = jnp.asarray(np.concatenate([ang_r, ang_r, ang_c, ang_c], axis=-1), F32)
    cos = jnp.concatenate([jnp.cos(ang), jnp.ones((n_ctx, QK_ROPE), F32)], axis=0)
    sin = jnp.concatenate([jnp.sin(ang), jnp.zeros((n_ctx, QK_ROPE), F32)], axis=0)
    return cos, sin


def kernel(x, c, ctx, c_ctx, ada_w, ada_b, norm_mix_w, norm_mlp_w, ab_w_in, ab_w_out, lru_conv_w, lru_conv_b, lru_w_a, lru_b_a, lru_w_x, lru_b_x, lru_lambda, mla_q_norm_w, mla_w_uq, mla_kv_norm_w, mla_w_ukv, hg_w_in, hg_lb_logits, hg_norm_w, hg_w_out, mlp_w1, mlp_w2, final_norm_w):
    n_batch, seq, d = x.shape
    n_ctx = ctx.shape[1]
    depth = ada_w.shape[0]
    assert n_batch + 1 <= SUBLANES

    xs = jnp.concatenate([x, ctx], axis=1)
    cond8 = jnp.zeros((SUBLANES, d), F32).at[:n_batch].set(c).at[n_batch].set(c_ctx)
    mods = _ada_all(cond8, ada_w, ada_b)
    lb_all = _lower_bounds(hg_lb_logits)
    cos_t, sin_t = _rope_tables(seq, n_ctx)

    d_lru = lru_conv_w.shape[-1]
    q_lora = mla_q_norm_w.shape[-1]
    for l in range(depth):
        if l % 2 == 0:
            e = l // 2
            w_in = ab_w_in[e]
            col_kr = w_in.shape[1] - QK_ROPE
            w_in_ext = jnp.concatenate([w_in, _rot_columns(w_in[:, col_kr:])], axis=1).astype(BF16)
            n_in = w_in_ext.shape[1]
            tn = n_in // 5 if (n_in // 5) % LANES == 0 and n_in % 5 == 0 else n_in
            p = _in_proj(xs, norm_mix_w[l], mods[l], w_in_ext, seq=seq, tn=tn)
            ya = _lru(p, lru_conv_w[e], lru_conv_b[e], lru_w_a[e].astype(BF16), lru_b_a[e],
                      lru_w_x[e].astype(BF16), lru_b_x[e], lru_lambda[e], seq=seq)
            wq = mla_w_uq[e].reshape(q_lora, MLA_HEADS, QK_NOPE + QK_ROPE)
            wq_ext = jnp.concatenate([wq, _rot_columns(wq[..., QK_NOPE:])], axis=-1)
            wq_ext = wq_ext.reshape(q_lora, -1).astype(BF16)
            q, k, v = _mla_qkv(p, mla_q_norm_w[e], mla_kv_norm_w[e], wq_ext, mla_w_ukv[e].astype(BF16),
                               cos_t, sin_t, seq=seq, col0=2 * d_lru)
            yb = _attention(q, k, v, seq=seq)
            xs = _out_proj_ab(ya, yb, ab_w_out[e].astype(BF16), xs, mods[l], seq=seq)
        else:
            o = l // 2
            w_in = hg_w_in[o].astype(BF16)
            tn = 1024 if w_in.shape[1] % 1024 == 0 else w_in.shape[1] // 5
            p = _in_proj(xs, norm_mix_w[l], mods[l], w_in, seq=seq, tn=tn)
            o_f, o_b = _hgrn2(p, lb_all[l], seq=seq)
            xs = _out_proj_hg(o_f, o_b, p, hg_norm_w[o], hg_w_out[o].astype(BF16), xs, mods[l], seq=seq)
        xs = _mlp(xs, norm_mlp_w[l], mods[l], mlp_w1[l].astype(BF16), mlp_w2[l].astype(BF16), seq=seq)
    return _final_norm(xs, final_norm_w, seq=seq)
```

```python
import functools

import numpy as np
import jax
import jax.numpy as jnp
from jax import lax
from jax.experimental import pallas as pl
from jax.experimental.pallas import tpu as pltpu

GRID_W = 64
EPS = 1e-6
N_MOD = 6
LRU_HEADS = 8
CONV_W = 4
CONV_LEFT = CONV_W // 2
LRU_C = 8.0
MLA_HEADS = 8
QK_NOPE = 128
QK_ROPE = 64
V_HEAD = 128
ROPE_BASE = 10000.0
HG_DK = 128
HG_CHUNK = 64

LANES = 128
SUBLANES = 8
VMEM_LIMIT_BYTES = 56 * 1024 * 1024

F32 = jnp.float32
BF16 = jnp.bfloat16

HG_LEVELS = int(np.log2(HG_CHUNK))
HG_EXP_ROWS = (2 + HG_LEVELS) * HG_CHUNK


def _cparams(*sem):
    return pltpu.CompilerParams(dimension_semantics=sem, vmem_limit_bytes=VMEM_LIMIT_BYTES)


def _dot(a, b):
    return jnp.dot(a, b, preferred_element_type=F32)


def _dot_nt(a, b):
    return lax.dot_general(a, b, (((1,), (1,)), ((), ())), preferred_element_type=F32)


def _dot_tn(a, b):
    return lax.dot_general(a, b, (((0,), (0,)), ((), ())), preferred_element_type=F32)


def _sigmoid(x):
    return 1.0 / (1.0 + jnp.exp(-x))


def _silu(x):
    return x * _sigmoid(x)


def _gelu_tanh(x):
    c = np.float32(np.sqrt(2.0 / np.pi))
    return 0.5 * x * (1.0 + jnp.tanh(c * (x + np.float32(0.044715) * (x * x * x))))


def _softplus(x):
    return jnp.maximum(x, 0.0) + jnp.log1p(jnp.exp(-jnp.abs(x)))


def _row_tile(t_total):
    tm = t_total // 8
    assert tm * 8 == t_total and tm % 16 == 0, t_total
    return tm


def _ada_kernel(c_ref, w_ref, b_ref, o_ref):
    a = _silu(c_ref[...]).astype(BF16)
    o_ref[...] = _dot(a, w_ref[...].astype(BF16)) + b_ref[...]


def _ada_all(cond8, ada_w, ada_b):
    n_layers, d, n6 = ada_w.shape
    tn = 1024 if n6 % 1024 == 0 else n6 // N_MOD
    return pl.pallas_call(
        _ada_kernel,
        out_shape=jax.ShapeDtypeStruct((n_layers, SUBLANES, n6), F32),
        grid=(n_layers, n6 // tn),
        in_specs=[
            pl.BlockSpec((SUBLANES, d), lambda l, j: (0, 0)),
            pl.BlockSpec((None, d, tn), lambda l, j: (l, 0, j)),
            pl.BlockSpec((None, 1, tn), lambda l, j: (l, 0, j)),
        ],
        out_specs=pl.BlockSpec((None, SUBLANES, tn), lambda l, j: (l, 0, j)),
        compiler_params=_cparams("arbitrary", "arbitrary"),
        name="ada_params",
    )(cond8, ada_w, ada_b.reshape(n_layers, 1, n6))


def _lb_kernel(x_ref, o_ref):
    n = x_ref.shape[0]
    rows = [x_ref[pl.ds(l, 1), :] for l in range(n)]
    mx = functools.reduce(jnp.maximum, rows)
    es = [jnp.exp(r - mx) for r in rows]
    tot = functools.reduce(lambda a, b: a + b, es)
    ps = [e / tot for e in es]
    acc = ps[0]
    o_ref[pl.ds(0, 1), :] = acc - ps[0]
    for l in range(1, n):
        acc = acc + ps[l]
        o_ref[pl.ds(l, 1), :] = acc - ps[0]


def _lower_bounds(logits):
    return pl.pallas_call(
        _lb_kernel, out_shape=jax.ShapeDtypeStruct(logits.shape, F32), name="hgrn2_lower_bounds"
    )(logits.astype(F32))


def _mod_rows(ref, b, n_batch, is_ctx):
    own = ref[pl.ds(b, 1), :]
    ctx = ref[pl.ds(n_batch, 1), :]
    return jnp.where(is_ctx, ctx, own)


def _is_ctx_rows(tm, seq):
    i = pl.program_id(1)
    row = i * tm + lax.broadcasted_iota(jnp.int32, (tm, 1), 0)
    return row >= seq


def _norm_modulate(x, nw, sh, sc):
    ms = jnp.mean(x * x, axis=-1, keepdims=True)
    y = x * lax.rsqrt(ms + EPS) * nw
    return y * (1.0 + sc) + sh


def _inproj_kernel(x_ref, nw_ref, sh_ref, sc_ref, w_ref, o_ref, h_ref, *, n_batch, seq):
    b = pl.program_id(0)
    tm = x_ref.shape[0]

    @pl.when(pl.program_id(2) == 0)
    def _():
        is_ctx = _is_ctx_rows(tm, seq)
        sh = _mod_rows(sh_ref, b, n_batch, is_ctx)
        sc = _mod_rows(sc_ref, b, n_batch, is_ctx)
        h_ref[...] = _norm_modulate(x_ref[...], nw_ref[...], sh, sc).astype(BF16)

    o_ref[...] = _dot(h_ref[...], w_ref[...]).astype(o_ref.dtype)


def _in_proj(xs, norm_w, mods, w, *, seq, tn, shift_idx=0, scale_idx=1):
    n_batch, t, d = xs.shape
    n_out = w.shape[1]
    tm = _row_tile(t)
    assert n_out % tn == 0
    kern = functools.partial(_inproj_kernel, n_batch=n_batch, seq=seq)
    return pl.pallas_call(
        kern,
        out_shape=jax.ShapeDtypeStruct((n_batch, t, n_out), F32),
        grid=(n_batch, t // tm, n_out // tn),
        in_specs=[
            pl.BlockSpec((None, tm, d), lambda b, i, j: (b, i, 0)),
            pl.BlockSpec((1, d), lambda b, i, j: (0, 0)),
            pl.BlockSpec((SUBLANES, d), lambda b, i, j: (0, shift_idx)),
            pl.BlockSpec((SUBLANES, d), lambda b, i, j: (0, scale_idx)),
            pl.BlockSpec((d, tn), lambda b, i, j: (0, j)),
        ],
        out_specs=pl.BlockSpec((None, tm, tn), lambda b, i, j: (b, i, j)),
        scratch_shapes=[pltpu.VMEM((tm, d), BF16)],
        compiler_params=_cparams("arbitrary", "arbitrary", "arbitrary"),
        name="in_proj",
    )(xs, norm_w.reshape(1, d), mods, mods, w)


def _mlp_kernel(x_ref, nw_ref, sh_ref, sc_ref, g_ref, w1_ref, w2_ref, o_ref, h_ref, acc_ref,
                *, n_batch, seq):
    b = pl.program_id(0)
    f = pl.program_id(2)
    tm = x_ref.shape[0]

    @pl.when(f == 0)
    def _():
        is_ctx = _is_ctx_rows(tm, seq)
        sh = _mod_rows(sh_ref, b, n_batch, is_ctx)
        sc = _mod_rows(sc_ref, b, n_batch, is_ctx)
        h_ref[...] = _norm_modulate(x_ref[...], nw_ref[...], sh, sc).astype(BF16)
        acc_ref[...] = jnp.zeros_like(acc_ref)

    a = jnp.maximum(_dot(h_ref[...], w1_ref[...]), 0.0)
    acc_ref[...] += _dot((a * a).astype(BF16), w2_ref[...])

    @pl.when(f == pl.num_programs(2) - 1)
    def _():
        is_ctx = _is_ctx_rows(tm, seq)
        g = _mod_rows(g_ref, b, n_batch, is_ctx)
        o_ref[...] = x_ref[...] + g * acc_ref[...]


def _mlp(xs, norm_w, mods, w1, w2, *, seq):
    n_batch, t, d = xs.shape
    d_ff = w1.shape[1]
    tm = _row_tile(t)
    tf = 1024 if d_ff % 1024 == 0 else d_ff
    kern = functools.partial(_mlp_kernel, n_batch=n_batch, seq=seq)
    return pl.pallas_call(
        kern,
        out_shape=jax.ShapeDtypeStruct((n_batch, t, d), F32),
        grid=(n_batch, t // tm, d_ff // tf),
        in_specs=[
            pl.BlockSpec((None, tm, d), lambda b, i, f: (b, i, 0)),
            pl.BlockSpec((1, d), lambda b, i, f: (0, 0)),
            pl.BlockSpec((SUBLANES, d), lambda b, i, f: (0, 3)),
            pl.BlockSpec((SUBLANES, d), lambda b, i, f: (0, 4)),
            pl.BlockSpec((SUBLANES, d), lambda b, i, f: (0, 5)),
            pl.BlockSpec((d, tf), lambda b, i, f: (0, f)),
            pl.BlockSpec((tf, d), lambda b, i, f: (f, 0)),
        ],
        out_specs=pl.BlockSpec((None, tm, d), lambda b, i, f: (b, i, 0)),
        scratch_shapes=[pltpu.VMEM((tm, d), BF16), pltpu.VMEM((tm, d), F32)],
        compiler_params=_cparams("arbitrary", "arbitrary", "arbitrary"),
        name="mlp",
    )(xs, norm_w.reshape(1, d), mods, mods, mods, w1, w2)


def _outproj_ab_kernel(ya_ref, yb_ref, w_ref, x_ref, g_ref, o_ref, *, n_batch, seq):
    b = pl.program_id(0)
    tm = x_ref.shape[0]
    da = ya_ref.shape[1]
    y = _dot(ya_ref[...], w_ref[0:da, :]) + _dot(yb_ref[...], w_ref[da:, :])
    g = _mod_rows(g_ref, b, n_batch, _is_ctx_rows(tm, seq))
    o_ref[...] = x_ref[...] + g * y


def _out_proj_ab(ya, yb, w_out, xs, mods, *, seq):
    n_batch, t, d = xs.shape
    da, db = ya.shape[2], yb.shape[2]
    tm = _row_tile(t)
    kern = functools.partial(_outproj_ab_kernel, n_batch=n_batch, seq=seq)
    return pl.pallas_call(
        kern,
        out_shape=jax.ShapeDtypeStruct((n_batch, t, d), F32),
        grid=(n_batch, t // tm),
        in_specs=[
            pl.BlockSpec((None, tm, da), lambda b, i: (b, i, 0)),
            pl.BlockSpec((None, tm, db), lambda b, i: (b, i, 0)),
            pl.BlockSpec((da + db, d), lambda b, i: (0, 0)),
            pl.BlockSpec((None, tm, d), lambda b, i: (b, i, 0)),
            pl.BlockSpec((SUBLANES, d), lambda b, i: (0, 2)),
        ],
        out_specs=pl.BlockSpec((None, tm, d), lambda b, i: (b, i, 0)),
        compiler_params=_cparams("arbitrary", "arbitrary"),
        name="out_proj_ab",
    )(ya, yb, w_out, xs, mods)


def _lru_kernel(gate_ref, u_ref, cw_ref, cb_ref, wa_ref, ba_ref, wx_ref, bx_ref, lam_ref, o_ref,
                upad_ref, af_ref, bf_ref, ab_ref, bb_ref, *, seq, chunk):
    t_total = u_ref.shape[0]
    n_ctx = t_total - seq
    pad = SUBLANES
    cw = cw_ref[...]
    cb = cb_ref[...]
    sp = _softplus(-lam_ref[...])
    zeros_pad = jnp.zeros((pad, LANES), F32)

    def gates(uc, d):
        ucb = uc.astype(BF16)
        r = _sigmoid(_dot(ucb, wa_ref[d]) + ba_ref[pl.ds(d, 1), :])
        i = _sigmoid(_dot(ucb, wx_ref[d]) + bx_ref[pl.ds(d, 1), :])
        log_a = (-LRU_C) * r * sp[d:d + 1, :]
        a = jnp.exp(log_a)
        return a, jnp.sqrt(1.0 - a * a) * (i * uc)

    def prepare(src0, dst0, n):
        upad_ref[pl.ds(0, pad), :] = zeros_pad
        upad_ref[pl.ds(pad + n, pad), :] = zeros_pad
        upad_ref[pl.ds(pad, n), :] = u_ref[pl.ds(src0, n), :]
        ch = min(chunk, n)

        def body(c, carry):
            r0 = pl.multiple_of(c * ch, SUBLANES)
            win = upad_ref[pl.ds(r0, ch + 2 * pad), :]
            uc = cb
            for j in range(CONV_W):
                off = pad + j - CONV_LEFT
                uc = uc + win[off:off + ch, :] * cw[j:j + 1, :]
            a0, b0 = gates(uc, 0)
            a1, b1 = gates(uc, 1)
            d0 = pl.multiple_of(dst0 + r0, SUBLANES)
            af_ref[pl.ds(d0, ch), :] = a0
            bf_ref[pl.ds(d0, ch), :] = b0
            ab_ref[pl.ds(d0, ch), :] = a1
            bb_ref[pl.ds(d0, ch), :] = b1
            return carry

        lax.fori_loop(0, n // ch, body, 0)

    prepare(seq, 0, n_ctx)
    prepare(0, n_ctx, seq)

    row = lax.broadcasted_iota(jnp.int32, (SUBLANES, LANES), 0)

    def scan_fwd(a, b):
        for s in (1, 2, 4):
            m = row >= s
            a_s = pltpu.roll(a, s, 0)
            b_s = pltpu.roll(b, s, 0)
            b = jnp.where(m, a * b_s + b, b)
            a = jnp.where(m, a * a_s, a)
        return a, b

    def scan_bwd(a, b):
        for s in (1, 2, 4):
            m = row < SUBLANES - s
            a_s = pltpu.roll(a, SUBLANES - s, 0)
            b_s = pltpu.roll(b, SUBLANES - s, 0)
            b = jnp.where(m, a * b_s + b, b)
            a = jnp.where(m, a * a_s, a)
        return a, b

    def scan_range(g0, n_groups, carry):
        def body(g, c):
            cf, cbk = c
            rf = pl.multiple_of((g0 + g) * SUBLANES, SUBLANES)
            a, b = scan_fwd(af_ref[pl.ds(rf, SUBLANES), :], bf_ref[pl.ds(rf, SUBLANES), :])
            hf = a * cf + b
            bf_ref[pl.ds(rf, SUBLANES), :] = hf
            rb = pl.multiple_of((g0 + n_groups - 1 - g) * SUBLANES, SUBLANES)
            a, b = scan_bwd(ab_ref[pl.ds(rb, SUBLANES), :], bb_ref[pl.ds(rb, SUBLANES), :])
            hb = a * cbk + b
            bb_ref[pl.ds(rb, SUBLANES), :] = hb
            cf = jnp.broadcast_to(hf[SUBLANES - 1:SUBLANES, :], (SUBLANES, LANES))
            cbk = jnp.broadcast_to(hb[0:1, :], (SUBLANES, LANES))
            return cf, cbk

        return lax.fori_loop(0, n_groups, body, carry, unroll=4)

    zero = jnp.zeros((SUBLANES, LANES), F32)
    carry = scan_range(0, n_ctx // SUBLANES, (zero, zero))
    scan_range(n_ctx // SUBLANES, seq // SUBLANES, carry)

    def finish(src0, dst0, n):
        ch = min(chunk, n)

        def body(c, carry):
            r0 = c * ch
            s0 = pl.multiple_of(src0 + r0, SUBLANES)
            d0 = pl.multiple_of(dst0 + r0, SUBLANES)
            h = bf_ref[pl.ds(s0, ch), :] + bb_ref[pl.ds(s0, ch), :]
            o_ref[pl.ds(d0, ch), :] = (h * _gelu_tanh(gate_ref[pl.ds(d0, ch), :])).astype(o_ref.dtype)
            return carry

        lax.fori_loop(0, n // ch, body, 0)

    finish(0, seq, n_ctx)
    finish(n_ctx, 0, seq)


def _lru(p, conv_w, conv_b, w_a, b_a, w_x, b_x, lam, *, seq):
    n_batch, t, _ = p.shape
    d_lru = conv_w.shape[1]
    n_heads = w_a.shape[1]
    assert d_lru // n_heads == LANES
    n_ctx = t - seq
    chunk = 256
    assert seq % min(chunk, seq) == 0 and n_ctx % min(chunk, n_ctx) == 0
    kern = functools.partial(_lru_kernel, seq=seq, chunk=chunk)
    vec = lambda rows: pl.BlockSpec((rows, LANES), lambda b, h: (0, h))
    wspec = pl.BlockSpec((2, None, LANES, LANES), lambda b, h: (0, h, 0, 0))
    scratch_rows = pltpu.VMEM((t, LANES), F32)
    return pl.pallas_call(
        kern,
        out_shape=jax.ShapeDtypeStruct((n_batch, t, d_lru), BF16),
        grid=(n_batch, n_heads),
        in_specs=[
            pl.BlockSpec((None, t, LANES), lambda b, h: (b, 0, h)),
            pl.BlockSpec((None, t, LANES), lambda b, h: (b, 0, n_heads + h)),
            vec(CONV_W), vec(1), wspec, vec(2), wspec, vec(2), vec(2),
        ],
        out_specs=pl.BlockSpec((None, t, LANES), lambda b, h: (b, 0, h)),
        scratch_shapes=[pltpu.VMEM((max(seq, n_ctx) + 2 * SUBLANES, LANES), F32),
                        scratch_rows, scratch_rows, scratch_rows, scratch_rows],
        compiler_params=_cparams("arbitrary", "arbitrary"),
        name="rglru",
    )(p, p, conv_w, conv_b.reshape(1, d_lru), w_a, b_a, w_x, b_x, lam)


def _mla_qkv_kernel(cq_ref, ckv_ref, kr_ref, qn_ref, kvn_ref, wq_ref, wkv_ref, cos_ref, sin_ref,
                    q_ref, k_ref, v_ref, *, scale):
    def rms(x, w):
        return (x * lax.rsqrt(jnp.mean(x * x, axis=-1, keepdims=True) + EPS) * w).astype(BF16)

    cos = cos_ref[...]
    sin = sin_ref[...]
    q_all = _dot(rms(cq_ref[...], qn_ref[...]), wq_ref[...])
    kv_all = _dot(rms(ckv_ref[...], kvn_ref[...]), wkv_ref[...])
    kr = kr_ref[...]
    kr_rot = kr[:, :QK_ROPE] * cos + kr[:, QK_ROPE:] * sin
    wq_head = QK_NOPE + 2 * QK_ROPE
    wkv_head = QK_NOPE + V_HEAD
    for h in range(q_ref.shape[0]):
        qh = q_all[:, h * wq_head:(h + 1) * wq_head]
        qr = qh[:, QK_NOPE:QK_NOPE + QK_ROPE] * cos + qh[:, QK_NOPE + QK_ROPE:] * sin
        q_ref[h, :, 0:QK_NOPE] = (qh[:, :QK_NOPE] * scale).astype(BF16)
        q_ref[h, :, QK_NOPE:] = (qr * scale).astype(BF16)
        kvh = kv_all[:, h * wkv_head:(h + 1) * wkv_head]
        k_ref[h, :, 0:QK_NOPE] = kvh[:, :QK_NOPE].astype(BF16)
        k_ref[h, :, QK_NOPE:] = kr_rot.astype(BF16)
        v_ref[h] = kvh[:, QK_NOPE:].T.astype(BF16)


def _mla_qkv(p, q_norm_w, kv_norm_w, wq_ext, wkv, cos_t, sin_t, *, seq, col0):
    n_batch, t, _ = p.shape
    q_lora = q_norm_w.shape[0]
    kv_lora = kv_norm_w.shape[0]
    assert q_lora == kv_lora and col0 % q_lora == 0 and (col0 + 2 * q_lora) % LANES == 0
    n_ctx = t - seq
    tm = n_ctx
    nh = MLA_HEADS
    dqk = QK_NOPE + QK_ROPE
    scale = np.float32(dqk ** -0.5 * np.log2(np.e))
    cq_blk = col0 // q_lora
    kr_blk = (col0 + 2 * q_lora) // LANES
    full = lambda shape: pl.BlockSpec(shape, lambda b, i: (0,) * len(shape))
    return pl.pallas_call(
        functools.partial(_mla_qkv_kernel, scale=scale),
        out_shape=(jax.ShapeDtypeStruct((n_batch, nh, t, dqk), BF16),
                   jax.ShapeDtypeStruct((n_batch, nh, t, dqk), BF16),
                   jax.ShapeDtypeStruct((n_batch, nh, V_HEAD, t), BF16)),
        grid=(n_batch, t // tm),
        in_specs=[
            pl.BlockSpec((None, tm, q_lora), lambda b, i: (b, i, cq_blk)),
            pl.BlockSpec((None, tm, kv_lora), lambda b, i: (b, i, cq_blk + 1)),
            pl.BlockSpec((None, tm, LANES), lambda b, i: (b, i, kr_blk)),
            full((1, q_lora)), full((1, kv_lora)), full(wq_ext.shape), full(wkv.shape),
            pl.BlockSpec((tm, QK_ROPE), lambda b, i: (i, 0)),
            pl.BlockSpec((tm, QK_ROPE), lambda b, i: (i, 0)),
        ],
        out_specs=(pl.BlockSpec((None, nh, tm, dqk), lambda b, i: (b, 0, i, 0)),
                   pl.BlockSpec((None, nh, tm, dqk), lambda b, i: (b, 0, i, 0)),
                   pl.BlockSpec((None, nh, V_HEAD, tm), lambda b, i: (b, 0, 0, i))),
        compiler_params=_cparams("arbitrary", "arbitrary"),
        name="mla_qkv",
    )(p, p, p, q_norm_w.reshape(1, q_lora), kv_norm_w.reshape(1, kv_lora), wq_ext, wkv, cos_t, sin_t)


def _attn_kernel(q_ref, k_ref, vt_ref, o_ref, *, seq):
    q = q_ref[...]
    t = k_ref.shape[0]

    def attend(k0):
        s = _dot_nt(k_ref[k0:, :], q)
        m = jnp.max(s, axis=0, keepdims=True)
        p = jnp.exp2(s - m)
        l = jnp.sum(p, axis=0, keepdims=True)
        o_t = _dot(vt_ref[:, k0:], p.astype(BF16))
        o_ref[...] = (o_t * (1.0 / l)).T.astype(o_ref.dtype)

    is_ctx_tile = pl.program_id(2) == pl.num_programs(2) - 1

    @pl.when(is_ctx_tile)
    def _():
        attend(seq)

    @pl.when(jnp.logical_not(is_ctx_tile))
    def _():
        attend(0)


def _attention(q, k, vt, *, seq):
    n_batch, nh, t, dqk = q.shape
    tq = t - seq
    assert seq % tq == 0
    return pl.pallas_call(
        functools.partial(_attn_kernel, seq=seq),
        out_shape=jax.ShapeDtypeStruct((n_batch, t, nh * V_HEAD), BF16),
        grid=(n_batch, nh, t // tq),
        in_specs=[
            pl.BlockSpec((None, None, tq, dqk), lambda b, h, i: (b, h, i, 0)),
            pl.BlockSpec((None, None, t, dqk), lambda b, h, i: (b, h, 0, 0)),
            pl.BlockSpec((None, None, V_HEAD, t), lambda b, h, i: (b, h, 0, 0)),
        ],
        out_specs=pl.BlockSpec((None, tq, V_HEAD), lambda b, h, i: (b, i, h)),
        compiler_params=_cparams("arbitrary", "arbitrary", "arbitrary"),
        name="mla_attention",
    )(q, k, vt)


def _hgrn2_tables(tt):
    n = HG_CHUNK
    t = np.arange(n)[:, None]
    r = np.arange(n)[None, :]
    fwd = [(r <= t), (r > t)]
    bwd = [(r >= t), (r < t)]
    for lvl in range(HG_LEVELS):
        m = 1 << lvl
        e = (t >> (lvl + 1)) * 2 * m + m - 1
        right = ((t >> lvl) & 1) == 1
        fwd.append(np.where(right, (r > e) & (r <= t), (r > t) & (r <= e)))
        bwd.append(np.where(right, (r > e) & (r < t), (r >= t) & (r <= e)))
    stack = lambda ms: np.tile(np.concatenate(ms, axis=0).astype(np.float32), (1, 2))
    tq = np.arange(tt)[:, None]
    ts = np.arange(tt)[None, :]
    level = np.floor(np.log2(np.maximum(tq ^ ts, 1))).astype(np.int32)
    lv_f = np.where(ts < tq, level, -1).astype(np.int32)
    return stack(fwd), stack(bwd), lv_f, np.ascontiguousarray(lv_f.T)


def _hgrn2_kernel(qf_ref, ff_ref, vf_ref, qb_ref, fb_ref, vb_ref, lb_ref, mf_ref, mb_ref, lvf_ref,
                  lvb_ref, of_ref, ob_ref, sf_ref, sb_ref, *, heads):
    n = HG_CHUNK
    tt = qf_ref.shape[0]
    wb = qf_ref.shape[1]
    nc = tt // n
    n_levels = HG_LEVELS + int(np.log2(nc))

    @pl.when(pl.program_id(2) == 0)
    def _():
        sf_ref[...] = jnp.zeros_like(sf_ref)
        sb_ref[...] = jnp.zeros_like(sb_ref)

    lb = lb_ref[...]
    row = lax.broadcasted_iota(jnp.int32, (tt, 1), 0)

    def prod(vs):
        return functools.reduce(lambda a, b: a * b, vs) if vs else None

    def by_chunk(pieces):
        return jnp.concatenate(
            [x[c * n:(c + 1) * n] if m is None else x[c * n:(c + 1) * n] * m
             for c, (x, m) in enumerate(pieces)], axis=0)

    def one_direction(q_ref, f_ref, v_ref, m_ref, lv_ref, o_ref, s_ref, backward):
        f = lb + (1.0 - lb) * _sigmoid(f_ref[...])
        g = jnp.log(f)
        g_hi = g.astype(BF16)
        g_lo = (g - g_hi.astype(F32)).astype(BF16)
        lanes = lambda x: jnp.concatenate([x[c * n:(c + 1) * n] for c in range(nc)], axis=1)
        g2 = jnp.concatenate([lanes(g_hi), lanes(g_lo)], axis=0)
        x_all = jnp.exp(_dot(m_ref[...], g2))
        q_all = _silu(q_ref[...])
        k_all = 1.0 - f
        v_all = v_ref[...]
        lv = lv_ref[...]
        outs = []
        for h in range(heads):
            cs = slice(h * HG_DK, (h + 1) * HG_DK)
            q, k, v = q_all[:, cs], k_all[:, cs], v_all[:, cs]
            vb = v.astype(BF16)

            def xblk(r):
                return jnp.concatenate(
                    [x_all[r * n:(r + 1) * n, c * wb + h * HG_DK:c * wb + (h + 1) * HG_DK]
                     for c in range(nc)], axis=0)

            eq_c = xblk(0)
            qe = q * eq_c
            ke = k * xblk(1)
            edge = (lambda c: c * n) if backward else (lambda c: c * n + n - 1)
            tot = [eq_c[edge(c):edge(c) + 1] for c in range(nc)]
            before = lambda c, lo: prod([tot[j] for j in range(lo, c)])
            after = lambda c, hi: prod([tot[j] for j in range(c + 1, hi)])

            a = jnp.zeros((tt, tt), F32)
            for lvl in range(n_levels):
                if lvl < HG_LEVELS:
                    is_q = ((row >> lvl) & 1) == (0 if backward else 1)
                    z = xblk(2 + lvl) * jnp.where(is_q, q, k)
                else:
                    mc = 1 << (lvl - HG_LEVELS)
                    pieces = []
                    for c in range(nc):
                        lo = (c // mc) * mc
                        is_left = ((c // mc) & 1) == 0
                        if is_left == backward:
                            pieces.append((qe, after(c, lo + mc) if backward else before(c, lo)))
                        else:
                            pieces.append((ke, before(c, lo) if backward else after(c, lo + mc)))
                    z = by_chunk(pieces)
                zb = z.astype(BF16)
                a = jnp.where(lv == lvl, _dot_nt(zb, zb), a)

            if backward:
                qe_t = by_chunk([(qe, after(c, nc)) for c in range(nc)])
                ke_t = by_chunk([(ke, before(c, 0)) for c in range(nc)])
            else:
                qe_t = by_chunk([(qe, before(c, 0)) for c in range(nc)])
                ke_t = by_chunk([(ke, after(c, nc)) for c in range(nc)])
            s_t = s_ref[h]
            o = _dot_nt(qe_t.astype(BF16), s_t.astype(BF16)) + _dot(a.astype(BF16), vb)
            o = o + jnp.sum(q * k, axis=-1, keepdims=True) * v
            s_ref[h] = s_t * prod(tot) + _dot_tn(vb, ke_t.astype(BF16))
            outs.append(o)
        o_ref[...] = outs[0] if heads == 1 else jnp.concatenate(outs, axis=1)

    one_direction(qf_ref, ff_ref, vf_ref, mf_ref, lvf_ref, of_ref, sf_ref, False)
    one_direction(qb_ref, fb_ref, vb_ref, mb_ref, lvb_ref, ob_ref, sb_ref, True)


def _hgrn2(p, lb, *, seq, heads_per_step=2):
    n_batch, t, n_in = p.shape
    d_k = lb.shape[-1]
    n_heads = d_k // HG_DK
    hb = heads_per_step if n_heads % heads_per_step == 0 else 1
    wb = hb * HG_DK
    tt = t - seq
    assert seq % tt == 0 and tt % HG_CHUNK == 0 and n_in == 5 * d_k
    assert (tt // HG_CHUNK) & (tt // HG_CHUNK - 1) == 0
    nt = t // tt
    ncol = d_k // wb
    mf, mb, lvf, lvb = _hgrn2_tables(tt)

    def tile_f(s):
        return jnp.where(s == 0, nt - 1, s - 1)

    def tile_b(s):
        return jnp.where(s == 0, nt - 1, nt - 1 - s)

    def col(tile, grp):
        return pl.BlockSpec((None, tt, wb), lambda b, g, s: (b, tile(s), grp * ncol + g))

    const = lambda shape: pl.BlockSpec(shape, lambda b, g, s: (0, 0))
    out_sd = jax.ShapeDtypeStruct((n_batch, t, d_k), F32)
    return pl.pallas_call(
        functools.partial(_hgrn2_kernel, heads=hb),
        out_shape=(out_sd, out_sd),
        grid=(n_batch, ncol, nt),
        in_specs=[col(tile_f, 0), col(tile_f, 1), col(tile_f, 3),
                  col(tile_b, 0), col(tile_b, 2), col(tile_b, 3),
                  pl.BlockSpec((1, wb), lambda b, g, s: (0, g)),
                  const(mf.shape), const(mb.shape), const(lvf.shape), const(lvb.shape)],
        out_specs=(pl.BlockSpec((None, tt, wb), lambda b, g, s: (b, tile_f(s), g)),
                   pl.BlockSpec((None, tt, wb), lambda b, g, s: (b, tile_b(s), g))),
        scratch_shapes=[pltpu.VMEM((hb, HG_DK, HG_DK), F32), pltpu.VMEM((hb, HG_DK, HG_DK), F32)],
        compiler_params=_cparams("arbitrary", "arbitrary", "arbitrary"),
        name="hgrn2_scan",
    )(p, p, p, p, p, p, lb.reshape(1, d_k), jnp.asarray(mf, BF16), jnp.asarray(mb, BF16),
      jnp.asarray(lvf), jnp.asarray(lvb))


def _outproj_hg_kernel(of_ref, ob_ref, gate_ref, nw_ref, w_ref, x_ref, g_ref, o_ref, y_ref,
                       *, n_batch, seq):
    b = pl.program_id(0)
    tm = x_ref.shape[0]
    nw = nw_ref[...]
    dv = nw.shape[1]
    for h in range(of_ref.shape[1] // dv):
        cs = slice(h * dv, (h + 1) * dv)
        o = of_ref[:, cs] + ob_ref[:, cs]
        y = o * lax.rsqrt(jnp.mean(o * o, axis=-1, keepdims=True) + EPS) * nw
        y_ref[:, cs] = (y * _silu(gate_ref[:, cs])).astype(BF16)
    g = _mod_rows(g_ref, b, n_batch, _is_ctx_rows(tm, seq))
    o_ref[...] = x_ref[...] + g * _dot(y_ref[...], w_ref[...])


def _out_proj_hg(o_f, o_b, p, norm_w, w_out, xs, mods, *, seq):
    n_batch, t, d = xs.shape
    d_v = o_f.shape[2]
    tm = _row_tile(t) // 2
    assert tm % 16 == 0
    gate_blk = (p.shape[2] - d_v) // d_v
    kern = functools.partial(_outproj_hg_kernel, n_batch=n_batch, seq=seq)
    rows = lambda width, cblk: pl.BlockSpec((None, tm, width), lambda b, i: (b, i, cblk))
    return pl.pallas_call(
        kern,
        out_shape=jax.ShapeDtypeStruct((n_batch, t, d), F32),
        grid=(n_batch, t // tm),
        in_specs=[
            rows(d_v, 0), rows(d_v, 0), rows(d_v, gate_blk),
            pl.BlockSpec((1, norm_w.shape[0]), lambda b, i: (0, 0)),
            pl.BlockSpec((d_v, d), lambda b, i: (0, 0)),
            rows(d, 0),
            pl.BlockSpec((SUBLANES, d), lambda b, i: (0, 2)),
        ],
        out_specs=rows(d, 0),
        scratch_shapes=[pltpu.VMEM((tm, d_v), BF16)],
        compiler_params=_cparams("arbitrary", "arbitrary"),
        name="out_proj_hgrn2",
    )(o_f, o_b, p, norm_w.reshape(1, -1), w_out, xs, mods)


def _final_norm_kernel(x_ref, w_ref, o_ref):
    x = x_ref[...]
    o_ref[...] = x * lax.rsqrt(jnp.mean(x * x, axis=-1, keepdims=True) + EPS) * w_ref[...]


def _final_norm(xs, w, *, seq):
    n_batch, t, d = xs.shape
    tm = t - seq
    return pl.pallas_call(
        _final_norm_kernel,
        out_shape=jax.ShapeDtypeStruct((n_batch, seq, d), F32),
        grid=(n_batch, seq // tm),
        in_specs=[pl.BlockSpec((None, tm, d), lambda b, i: (b, i, 0)),
                  pl.BlockSpec((1, d), lambda b, i: (0, 0))],
        out_specs=pl.BlockSpec((None, tm, d), lambda b, i: (b, i, 0)),
        compiler_params=_cparams("arbitrary", "arbitrary"),
        name="final_norm",
    )(xs, w.reshape(1, d))


def _rot_columns(w):
    q = QK_ROPE // 4
    return jnp.concatenate([-w[..., q:2 * q], w[..., :q], -w[..., 3 * q:], w[..., 2 * q:3 * q]], axis=-1)


def _rope_tables(seq, n_ctx):
    t = np.arange(seq)
    row, colm = t // GRID_W, t % GRID_W
    half = QK_ROPE // 2
    inv_freq = ROPE_BASE ** (-np.arange(0, half, 2, dtype=np.float32) / half)
    ang_r = row.astype(np.float32)[:, None] * inv_freq
    ang_c = colm.astype(np.float32)[:, None] * inv_freq
    ang = jnp.asarray(np.concatenate([ang_r, ang_r, ang_c, ang_c], axis=-1), F32)
    cos = jnp.concatenate([jnp.cos(ang), jnp.ones((n_ctx, QK_ROPE), F32)], axis=0)
    sin = jnp.concatenate([jnp.sin(ang), jnp.zeros((n_ctx, QK_ROPE), F32)], axis=0)
    return cos, sin


def kernel(x, c, ctx, c_ctx, ada_w, ada_b, norm_mix_w, norm_mlp_w, ab_w_in, ab_w_out, lru_conv_w, lru_conv_b, lru_w_a, lru_b_a, lru_w_x, lru_b_x, lru_lambda, mla_q_norm_w, mla_w_uq, mla_kv_norm_w, mla_w_ukv, hg_w_in, hg_lb_logits, hg_norm_w, hg_w_out, mlp_w1, mlp_w2, final_norm_w):
    n_batch, seq, d = x.shape
    n_ctx = ctx.shape[1]
    depth = ada_w.shape[0]
    assert n_batch + 1 <= SUBLANES

    xs = jnp.concatenate([x, ctx], axis=1)
    cond8 = jnp.zeros((SUBLANES, d), F32).at[:n_batch].set(c).at[n_batch].set(c_ctx)
    mods = _ada_all(cond8, ada_w, ada_b)
    lb_all = _lower_bounds(hg_lb_logits)
    cos_t, sin_t = _rope_tables(seq, n_ctx)

    d_lru = lru_conv_w.shape[-1]
    q_lora = mla_q_norm_w.shape[-1]
    for l in range(depth):
        if l % 2 == 0:
            e = l // 2
            w_in = ab_w_in[e]
            col_kr = w_in.shape[1] - QK_ROPE
            w_in_ext = jnp.concatenate([w_in, _rot_columns(w_in[:, col_kr:])], axis=1).astype(BF16)
            n_in = w_in_ext.shape[1]
            tn = n_in // 5 if (n_in // 5) % LANES == 0 and n_in % 5 == 0 else n_in
            p = _in_proj(xs, norm_mix_w[l], mods[l], w_in_ext, seq=seq, tn=tn)
            ya = _lru(p, lru_conv_w[e], lru_conv_b[e], lru_w_a[e].astype(BF16), lru_b_a[e],
                      lru_w_x[e].astype(BF16), lru_b_x[e], lru_lambda[e], seq=seq)
            wq = mla_w_uq[e].reshape(q_lora, MLA_HEADS, QK_NOPE + QK_ROPE)
            wq_ext = jnp.concatenate([wq, _rot_columns(wq[..., QK_NOPE:])], axis=-1)
            wq_ext = wq_ext.reshape(q_lora, -1).astype(BF16)
            q, k, v = _mla_qkv(p, mla_q_norm_w[e], mla_kv_norm_w[e], wq_ext, mla_w_ukv[e].astype(BF16),
                               cos_t, sin_t, seq=seq, col0=2 * d_lru)
            yb = _attention(q, k, v, seq=seq)
            xs = _out_proj_ab(ya, yb, ab_w_out[e].astype(BF16), xs, mods[l], seq=seq)
        else:
            o = l // 2
            w_in = hg_w_in[o].astype(BF16)
            tn = 1024 if w_in.shape[1] % 1024 == 0 else w_in.shape[1] // 5
            p = _in_proj(xs, norm_mix_w[l], mods[l], w_in, seq=seq, tn=tn)
            o_f, o_b = _hgrn2(p, lb_all[l], seq=seq)
            xs = _out_proj_hg(o_f, o_b, p, hg_norm_w[o], hg_w_out[o].astype(BF16), xs, mods[l], seq=seq)
        xs = _mlp(xs, norm_mlp_w[l], mods[l], mlp_w1[l].astype(BF16), mlp_w2[l].astype(BF16), seq=seq)
    return _final_norm(xs, final_norm_w, seq=seq)
```

```python
import functools

import numpy as np
import jax
import jax.numpy as jnp
from jax import lax
from jax.experimental import pallas as pl
from jax.experimental.pallas import tpu as pltpu

GRID_W = 64
EPS = 1e-6
N_MOD = 6
LRU_HEADS = 8
CONV_W = 4
CONV_LEFT = CONV_W // 2
LRU_C = 8.0
MLA_HEADS = 8
QK_NOPE = 128
QK_ROPE = 64
V_HEAD = 128
ROPE_BASE = 10000.0
HG_DK = 128
HG_CHUNK = 64

LANES = 128
SUBLANES = 8
VMEM_LIMIT_BYTES = 56 * 1024 * 1024
NORM_ROW_CHUNK = 32
ATTN_KEY_CHUNK = 1024
ATTN_Q_TILE = 512

F32 = jnp.float32
BF16 = jnp.bfloat16

HG_LEVELS = int(np.log2(HG_CHUNK))


def _cparams(*sem):
    return pltpu.CompilerParams(dimension_semantics=sem, vmem_limit_bytes=VMEM_LIMIT_BYTES)


def _dot(a, b):
    return jnp.dot(a, b, preferred_element_type=F32)


def _dot_nt(a, b):
    return lax.dot_general(a, b, (((1,), (1,)), ((), ())), preferred_element_type=F32)


def _dot_tn(a, b):
    return lax.dot_general(a, b, (((0,), (0,)), ((), ())), preferred_element_type=F32)


def _sigmoid(x):
    return 1.0 / (1.0 + jnp.exp(-x))


def _silu(x):
    return x * _sigmoid(x)


def _gelu_tanh(x):
    c = np.float32(np.sqrt(2.0 / np.pi))
    return 0.5 * x * (1.0 + jnp.tanh(c * (x + np.float32(0.044715) * (x * x * x))))


def _softplus(x):
    return jnp.maximum(x, 0.0) + jnp.log1p(jnp.exp(-jnp.abs(x)))


def _row_tile(t_total):
    tm = t_total // 8
    assert tm * 8 == t_total and tm % 16 == 0, t_total
    return tm


def _ada_kernel(c_ref, w_ref, b_ref, o_ref):
    a = _silu(c_ref[...]).astype(BF16)
    o_ref[...] = _dot(a, w_ref[...].astype(BF16)) + b_ref[...]


def _ada_all(cond8, ada_w, ada_b):
    n_layers, d, n6 = ada_w.shape
    tn = 1024 if n6 % 1024 == 0 else n6 // N_MOD
    return pl.pallas_call(
        _ada_kernel,
        out_shape=jax.ShapeDtypeStruct((n_layers, SUBLANES, n6), F32),
        grid=(n_layers, n6 // tn),
        in_specs=[
            pl.BlockSpec((SUBLANES, d), lambda l, j: (0, 0)),
            pl.BlockSpec((None, d, tn), lambda l, j: (l, 0, j)),
            pl.BlockSpec((None, 1, tn), lambda l, j: (l, 0, j)),
        ],
        out_specs=pl.BlockSpec((None, SUBLANES, tn), lambda l, j: (l, 0, j)),
        compiler_params=_cparams("arbitrary", "arbitrary"),
        name="ada_params",
    )(cond8, ada_w, ada_b.reshape(n_layers, 1, n6))


def _lb_kernel(x_ref, o_ref):
    n = x_ref.shape[0]
    rows = [x_ref[pl.ds(l, 1), :] for l in range(n)]
    mx = functools.reduce(jnp.maximum, rows)
    es = [jnp.exp(r - mx) for r in rows]
    tot = functools.reduce(lambda a, b: a + b, es)
    ps = [e / tot for e in es]
    acc = ps[0]
    o_ref[pl.ds(0, 1), :] = acc - ps[0]
    for l in range(1, n):
        acc = acc + ps[l]
        o_ref[pl.ds(l, 1), :] = acc - ps[0]


def _lower_bounds(logits):
    return pl.pallas_call(
        _lb_kernel, out_shape=jax.ShapeDtypeStruct(logits.shape, F32), name="hgrn2_lower_bounds"
    )(logits.astype(F32))


def _for_row_groups(tm, t_total, seq, n_batch, fn):
    b = pl.program_id(0)
    i = pl.program_id(1)
    i_mixed, split = divmod(seq, tm)
    if i_mixed > 0:
        pl.when(i < i_mixed)(lambda: fn(0, tm, b))
    if i_mixed < t_total // tm:
        @pl.when(i == i_mixed)
        def _():
            if split > 0:
                fn(0, split, b)
            fn(split, tm, n_batch)
    if i_mixed + 1 < t_total // tm:
        pl.when(i > i_mixed)(lambda: fn(0, tm, n_batch))


def _norm_modulate_rows(x_ref, nw_ref, sh_ref, sc_ref, h_ref, r0, r1, rid):
    gain = nw_ref[...] * (1.0 + sc_ref[pl.ds(rid, 1), :])
    shift = sh_ref[pl.ds(rid, 1), :]
    rc = NORM_ROW_CHUNK
    for r in range(r0, r1, rc):
        x = x_ref[r:min(r + rc, r1), :]
        ms = jnp.mean(x * x, axis=-1, keepdims=True)
        h_ref[r:min(r + rc, r1), :] = (x * lax.rsqrt(ms + EPS) * gain + shift).astype(h_ref.dtype)


def _mod_spec(d, k, layer, nargs):
    if nargs == 2:
        return pl.BlockSpec((None, SUBLANES, d), lambda b, i: (layer, 0, k))
    return pl.BlockSpec((None, SUBLANES, d), lambda b, i, j: (layer, 0, k))


def _inproj_kernel(x_ref, nw_ref, sh_ref, sc_ref, w_ref, o_ref, h_ref, *, n_batch, seq, t_total):
    tm = x_ref.shape[0]

    @pl.when(pl.program_id(2) == 0)
    def _():
        _for_row_groups(tm, t_total, seq, n_batch, functools.partial(
            _norm_modulate_rows, x_ref, nw_ref, sh_ref, sc_ref, h_ref))

    o_ref[...] = _dot(h_ref[...], w_ref[...]).astype(o_ref.dtype)


def _in_proj(xs, norm_w, mods, w, layer, w_layer, *, seq, tn):
    n_batch, t, d = xs.shape
    n_out = w.shape[2]
    tm = _row_tile(t)
    assert n_out % tn == 0
    kern = functools.partial(_inproj_kernel, n_batch=n_batch, seq=seq, t_total=t)
    return pl.pallas_call(
        kern,
        out_shape=jax.ShapeDtypeStruct((n_batch, t, n_out), F32),
        grid=(n_batch, t // tm, n_out // tn),
        in_specs=[
            pl.BlockSpec((None, tm, d), lambda b, i, j: (b, i, 0)),
            pl.BlockSpec((1, d), lambda b, i, j: (0, 0)),
            _mod_spec(d, 0, layer, 3),
            _mod_spec(d, 1, layer, 3),
            pl.BlockSpec((None, d, tn), lambda b, i, j: (w_layer, 0, j)),
        ],
        out_specs=pl.BlockSpec((None, tm, tn), lambda b, i, j: (b, i, j)),
        scratch_shapes=[pltpu.VMEM((tm, d), BF16)],
        compiler_params=_cparams("arbitrary", "arbitrary", "arbitrary"),
        name="in_proj",
    )(xs, norm_w.reshape(1, d), mods, mods, w)


def _mlp_kernel(x_ref, nw_ref, sh_ref, sc_ref, g_ref, w1_ref, w2_ref, o_ref, h_ref, acc_ref,
                *, n_batch, seq, t_total):
    f = pl.program_id(2)
    tm = x_ref.shape[0]

    @pl.when(f == 0)
    def _():
        _for_row_groups(tm, t_total, seq, n_batch, functools.partial(
            _norm_modulate_rows, x_ref, nw_ref, sh_ref, sc_ref, h_ref))
        acc_ref[...] = jnp.zeros_like(acc_ref)

    a = jnp.maximum(_dot(h_ref[...], w1_ref[...]), 0.0)
    acc_ref[...] += _dot((a * a).astype(BF16), w2_ref[...])

    @pl.when(f == pl.num_programs(2) - 1)
    def _():
        def finish(r0, r1, rid):
            o_ref[r0:r1, :] = x_ref[r0:r1, :] + g_ref[pl.ds(rid, 1), :] * acc_ref[r0:r1, :]

        _for_row_groups(tm, t_total, seq, n_batch, finish)


def _mlp(xs, norm_w, mods, w1, w2, layer, *, seq):
    n_batch, t, d = xs.shape
    d_ff = w1.shape[2]
    tm = _row_tile(t)
    tf = 1024 if d_ff % 1024 == 0 else d_ff
    kern = functools.partial(_mlp_kernel, n_batch=n_batch, seq=seq, t_total=t)
    return pl.pallas_call(
        kern,
        out_shape=jax.ShapeDtypeStruct((n_batch, t, d), F32),
        grid=(n_batch, t // tm, d_ff // tf),
        in_specs=[
            pl.BlockSpec((None, tm, d), lambda b, i, f: (b, i, 0)),
            pl.BlockSpec((1, d), lambda b, i, f: (0, 0)),
            _mod_spec(d, 3, layer, 3),
            _mod_spec(d, 4, layer, 3),
            _mod_spec(d, 5, layer, 3),
            pl.BlockSpec((None, d, tf), lambda b, i, f: (layer, 0, f)),
            pl.BlockSpec((None, tf, d), lambda b, i, f: (layer, f, 0)),
        ],
        out_specs=pl.BlockSpec((None, tm, d), lambda b, i, f: (b, i, 0)),
        scratch_shapes=[pltpu.VMEM((tm, d), BF16), pltpu.VMEM((tm, d), F32)],
        compiler_params=_cparams("arbitrary", "arbitrary", "arbitrary"),
        name="mlp",
    )(xs, norm_w.reshape(1, d), mods, mods, mods, w1, w2)


def _outproj_ab_kernel(ya_ref, yb_ref, w_ref, x_ref, g_ref, o_ref, *, n_batch, seq, t_total):
    tm = x_ref.shape[0]
    da = ya_ref.shape[1]
    y = _dot(ya_ref[...], w_ref[0:da, :]) + _dot(yb_ref[...], w_ref[da:, :])

    def finish(r0, r1, rid):
        o_ref[r0:r1, :] = x_ref[r0:r1, :] + g_ref[pl.ds(rid, 1), :] * y[r0:r1, :]

    _for_row_groups(tm, t_total, seq, n_batch, finish)


def _out_proj_ab(ya, yb, w_out, xs, mods, layer, w_layer, *, seq):
    n_batch, t, d = xs.shape
    da, db = ya.shape[2], yb.shape[2]
    tm = _row_tile(t)
    kern = functools.partial(_outproj_ab_kernel, n_batch=n_batch, seq=seq, t_total=t)
    return pl.pallas_call(
        kern,
        out_shape=jax.ShapeDtypeStruct((n_batch, t, d), F32),
        grid=(n_batch, t // tm),
        in_specs=[
            pl.BlockSpec((None, tm, da), lambda b, i: (b, i, 0)),
            pl.BlockSpec((None, tm, db), lambda b, i: (b, i, 0)),
            pl.BlockSpec((None, da + db, d), lambda b, i: (w_layer, 0, 0)),
            pl.BlockSpec((None, tm, d), lambda b, i: (b, i, 0)),
            _mod_spec(d, 2, layer, 2),
        ],
        out_specs=pl.BlockSpec((None, tm, d), lambda b, i: (b, i, 0)),
        compiler_params=_cparams("arbitrary", "arbitrary"),
        name="out_proj_ab",
    )(ya, yb, w_out, xs, mods)


def _lru_kernel(gate_ref, u_ref, cw_ref, cb_ref, wa_ref, ba_ref, wx_ref, bx_ref, lam_ref, o_ref,
                upad_ref, af_ref, bf_ref, ab_ref, bb_ref, hf_ref, hb_ref, *, seq, chunk):
    t_total = u_ref.shape[0]
    n_ctx = t_total - seq
    pad = SUBLANES
    cw = cw_ref[...]
    cb = cb_ref[...]
    sp = _softplus(-lam_ref[...])
    zeros_pad = jnp.zeros((pad, LANES), F32)

    def gates(uc, d):
        ucb = uc.astype(BF16)
        r = _sigmoid(_dot(ucb, wa_ref[d]) + ba_ref[pl.ds(d, 1), :])
        i = _sigmoid(_dot(ucb, wx_ref[d]) + bx_ref[pl.ds(d, 1), :])
        log_a = (-LRU_C) * r * sp[d:d + 1, :]
        a = jnp.exp(log_a)
        return a, jnp.sqrt(1.0 - a * a) * (i * uc)

    def prepare(src0, dst0, n):
        upad_ref[pl.ds(0, pad), :] = zeros_pad
        upad_ref[pl.ds(pad + n, pad), :] = zeros_pad
        upad_ref[pl.ds(pad, n), :] = u_ref[pl.ds(src0, n), :]
        ch = min(chunk, n)

        def body(c, carry):
            r0 = pl.multiple_of(c * ch, SUBLANES)
            win = upad_ref[pl.ds(r0, ch + 2 * pad), :]
            uc = cb
            for j in range(CONV_W):
                off = pad + j - CONV_LEFT
                uc = uc + win[off:off + ch, :] * cw[j:j + 1, :]
            a0, b0 = gates(uc, 0)
            a1, b1 = gates(uc, 1)
            d0 = pl.multiple_of(dst0 + r0, SUBLANES)
            af_ref[pl.ds(d0, ch), :] = a0
            bf_ref[pl.ds(d0, ch), :] = b0
            ab_ref[pl.ds(d0, ch), :] = a1
            bb_ref[pl.ds(d0, ch), :] = b1
            return carry

        lax.fori_loop(0, n // ch, body, 0)

    prepare(seq, 0, n_ctx)
    prepare(0, n_ctx, seq)

    row = lax.broadcasted_iota(jnp.int32, (SUBLANES, LANES), 0)

    def scan_fwd(a, b):
        for s in (1, 2, 4):
            m = row >= s
            a_s = pltpu.roll(a, s, 0)
            b_s = pltpu.roll(b, s, 0)
            b = jnp.where(m, a * b_s + b, b)
            a = jnp.where(m, a * a_s, a)
        return a, b

    def scan_bwd(a, b):
        for s in (1, 2, 4):
            m = row < SUBLANES - s
            a_s = pltpu.roll(a, SUBLANES - s, 0)
            b_s = pltpu.roll(b, SUBLANES - s, 0)
            b = jnp.where(m, a * b_s + b, b)
            a = jnp.where(m, a * a_s, a)
        return a, b

    def scan_range(g0, n_groups, carry):
        def body(g, c):
            cf, cbk = c
            rf = pl.multiple_of((g0 + g) * SUBLANES, SUBLANES)
            a, b = scan_fwd(af_ref[pl.ds(rf, SUBLANES), :], bf_ref[pl.ds(rf, SUBLANES), :])
            hf = a * cf + b
            hf_ref[pl.ds(rf, SUBLANES), :] = hf
            rb = pl.multiple_of((g0 + n_groups - 1 - g) * SUBLANES, SUBLANES)
            a, b = scan_bwd(ab_ref[pl.ds(rb, SUBLANES), :], bb_ref[pl.ds(rb, SUBLANES), :])
            hb = a * cbk + b
            hb_ref[pl.ds(rb, SUBLANES), :] = hb
            cf = jnp.broadcast_to(hf[SUBLANES - 1:SUBLANES, :], (SUBLANES, LANES))
            cbk = jnp.broadcast_to(hb[0:1, :], (SUBLANES, LANES))
            return cf, cbk

        return lax.fori_loop(0, n_groups, body, carry, unroll=4)

    zero = jnp.zeros((SUBLANES, LANES), F32)
    carry = scan_range(0, n_ctx // SUBLANES, (zero, zero))
    scan_range(n_ctx // SUBLANES, seq // SUBLANES, carry)

    def finish(src0, dst0, n):
        ch = min(chunk, n)

        def body(c, carry):
            r0 = c * ch
            s0 = pl.multiple_of(src0 + r0, SUBLANES)
            d0 = pl.multiple_of(dst0 + r0, SUBLANES)
            h = hf_ref[pl.ds(s0, ch), :] + hb_ref[pl.ds(s0, ch), :]
            o_ref[pl.ds(d0, ch), :] = (h * _gelu_tanh(gate_ref[pl.ds(d0, ch), :])).astype(o_ref.dtype)
            return carry

        lax.fori_loop(0, n // ch, body, 0)

    finish(0, seq, n_ctx)
    finish(n_ctx, 0, seq)


def _lru(p, conv_w, conv_b, w_a, b_a, w_x, b_x, lam, *, seq):
    n_batch, t, _ = p.shape
    d_lru = conv_w.shape[1]
    n_heads = w_a.shape[1]
    assert d_lru // n_heads == LANES
    n_ctx = t - seq
    chunk = 256
    assert seq % min(chunk, seq) == 0 and n_ctx % min(chunk, n_ctx) == 0
    kern = functools.partial(_lru_kernel, seq=seq, chunk=chunk)
    vec = lambda rows: pl.BlockSpec((rows, LANES), lambda b, h: (0, h))
    wspec = pl.BlockSpec((2, None, LANES, LANES), lambda b, h: (0, h, 0, 0))
    scratch_rows = pltpu.VMEM((t, LANES), F32)
    return pl.pallas_call(
        kern,
        out_shape=jax.ShapeDtypeStruct((n_batch, t, d_lru), BF16),
        grid=(n_batch, n_heads),
        in_specs=[
            pl.BlockSpec((None, t, LANES), lambda b, h: (b, 0, h)),
            pl.BlockSpec((None, t, LANES), lambda b, h: (b, 0, n_heads + h)),
            vec(CONV_W), vec(1), wspec, vec(2), wspec, vec(2), vec(2),
        ],
        out_specs=pl.BlockSpec((None, t, LANES), lambda b, h: (b, 0, h)),
        scratch_shapes=[pltpu.VMEM((max(seq, n_ctx) + 2 * SUBLANES, LANES), F32),
                        scratch_rows, scratch_rows, scratch_rows, scratch_rows,
                        scratch_rows, scratch_rows],
        compiler_params=_cparams("arbitrary", "arbitrary"),
        name="rglru",
    )(p, p, conv_w, conv_b.reshape(1, d_lru), w_a, b_a, w_x, b_x, lam)


def _mla_qkv_kernel(cq_ref, ckv_ref, kr_ref, qn_ref, kvn_ref, wq_ref, wkv_ref, cos_ref, sin_ref,
                    q_ref, k_ref, v_ref, *, scale):
    def rms(x, w):
        return (x * lax.rsqrt(jnp.mean(x * x, axis=-1, keepdims=True) + EPS) * w).astype(BF16)

    cos = cos_ref[...]
    sin = sin_ref[...]
    q_all = _dot(rms(cq_ref[...], qn_ref[...]), wq_ref[...])
    kv_all = _dot(rms(ckv_ref[...], kvn_ref[...]), wkv_ref[...])
    kr = kr_ref[...]
    kr_rot = kr[:, :QK_ROPE] * cos + kr[:, QK_ROPE:] * sin
    wq_head = QK_NOPE + 2 * QK_ROPE
    wkv_head = QK_NOPE + V_HEAD
    for h in range(q_ref.shape[0]):
        qh = q_all[:, h * wq_head:(h + 1) * wq_head]
        qr = qh[:, QK_NOPE:QK_NOPE + QK_ROPE] * cos + qh[:, QK_NOPE + QK_ROPE:] * sin
        q_ref[h, :, 0:QK_NOPE] = (qh[:, :QK_NOPE] * scale).astype(BF16)
        q_ref[h, :, QK_NOPE:] = (qr * scale).astype(BF16)
        kvh = kv_all[:, h * wkv_head:(h + 1) * wkv_head]
        k_ref[h, :, 0:QK_NOPE] = kvh[:, :QK_NOPE].astype(BF16)
        k_ref[h, :, QK_NOPE:] = kr_rot.astype(BF16)
        v_ref[h] = kvh[:, QK_NOPE:].T.astype(BF16)


def _mla_qkv(p, q_norm_w, kv_norm_w, wq_ext, wkv, cos_t, sin_t, *, seq, col0):
    n_batch, t, _ = p.shape
    q_lora = q_norm_w.shape[0]
    kv_lora = kv_norm_w.shape[0]
    assert q_lora == kv_lora and col0 % q_lora == 0 and (col0 + 2 * q_lora) % LANES == 0
    n_ctx = t - seq
    tm = n_ctx
    nh = MLA_HEADS
    dqk = QK_NOPE + QK_ROPE
    scale = np.float32(dqk ** -0.5 * np.log2(np.e))
    cq_blk = col0 // q_lora
    kr_blk = (col0 + 2 * q_lora) // LANES
    full = lambda shape: pl.BlockSpec(shape, lambda b, i: (0,) * len(shape))
    return pl.pallas_call(
        functools.partial(_mla_qkv_kernel, scale=scale),
        out_shape=(jax.ShapeDtypeStruct((n_batch, nh, t, dqk), BF16),
                   jax.ShapeDtypeStruct((n_batch, nh, t, dqk), BF16),
                   jax.ShapeDtypeStruct((n_batch, nh, V_HEAD, t), BF16)),
        grid=(n_batch, t // tm),
        in_specs=[
            pl.BlockSpec((None, tm, q_lora), lambda b, i: (b, i, cq_blk)),
            pl.BlockSpec((None, tm, kv_lora), lambda b, i: (b, i, cq_blk + 1)),
            pl.BlockSpec((None, tm, LANES), lambda b, i: (b, i, kr_blk)),
            full((1, q_lora)), full((1, kv_lora)), full(wq_ext.shape), full(wkv.shape),
            pl.BlockSpec((tm, QK_ROPE), lambda b, i: (i, 0)),
            pl.BlockSpec((tm, QK_ROPE), lambda b, i: (i, 0)),
        ],
        out_specs=(pl.BlockSpec((None, nh, tm, dqk), lambda b, i: (b, 0, i, 0)),
                   pl.BlockSpec((None, nh, tm, dqk), lambda b, i: (b, 0, i, 0)),
                   pl.BlockSpec((None, nh, V_HEAD, tm), lambda b, i: (b, 0, 0, i))),
        compiler_params=_cparams("arbitrary", "arbitrary"),
        name="mla_qkv",
    )(p, p, p, q_norm_w.reshape(1, q_lora), kv_norm_w.reshape(1, kv_lora), wq_ext, wkv, cos_t, sin_t)


def _softmax_values(q, k_ref, vt_ref, s_ref):
    t = k_ref.shape[0]
    bounds = list(range(0, t, ATTN_KEY_CHUNK)) + [t]
    chunks = list(zip(bounds[:-1], bounds[1:]))
    m = None
    for a, b in chunks:
        s = _dot_nt(k_ref[a:b, :], q)
        s_ref[a:b, :] = s
        cm = jnp.max(s, axis=0, keepdims=True)
        m = cm if m is None else jnp.maximum(m, cm)
    l = jnp.zeros_like(m)
    o_t = jnp.zeros((vt_ref.shape[0], q.shape[0]), F32)
    for a, b in chunks:
        p = jnp.exp2(s_ref[a:b, :] - m)
        l = l + jnp.sum(p, axis=0, keepdims=True)
        o_t = o_t + _dot(vt_ref[:, a:b], p.astype(BF16))
    return (o_t * (1.0 / l)).T


def _attn_latent_kernel(q_ref, k_ref, vt_ref, o_ref, s_ref):
    o_ref[...] = _softmax_values(q_ref[...], k_ref, vt_ref, s_ref).astype(o_ref.dtype)


def _attn_ctx_kernel(q_ref, k_ref, vt_ref, y_hbm_ref, o_ref, s_ref):
    del y_hbm_ref
    o_ref[...] = _softmax_values(q_ref[...], k_ref, vt_ref, s_ref).astype(o_ref.dtype)


def _attention(q, k, vt, *, seq):
    n_batch, nh, t, dqk = q.shape
    n_ctx = t - seq
    tq = ATTN_Q_TILE if seq % ATTN_Q_TILE == 0 else n_ctx
    assert seq % tq == 0 and seq % n_ctx == 0
    y = pl.pallas_call(
        _attn_latent_kernel,
        out_shape=jax.ShapeDtypeStruct((n_batch, t, nh * V_HEAD), BF16),
        grid=(n_batch, nh, seq // tq),
        in_specs=[
            pl.BlockSpec((None, None, tq, dqk), lambda b, h, i: (b, h, i, 0)),
            pl.BlockSpec((None, None, t, dqk), lambda b, h, i: (b, h, 0, 0)),
            pl.BlockSpec((None, None, V_HEAD, t), lambda b, h, i: (b, h, 0, 0)),
        ],
        out_specs=pl.BlockSpec((None, tq, V_HEAD), lambda b, h, i: (b, i, h)),
        scratch_shapes=[pltpu.VMEM((t, tq), F32)],
        compiler_params=_cparams("arbitrary", "arbitrary", "arbitrary"),
        name="mla_attention",
    )(q, k, vt)
    ctx_blk = seq // n_ctx
    return pl.pallas_call(
        _attn_ctx_kernel,
        out_shape=jax.ShapeDtypeStruct(y.shape, y.dtype),
        grid=(n_batch, nh),
        in_specs=[
            pl.BlockSpec((None, None, n_ctx, dqk), lambda b, h: (b, h, ctx_blk, 0)),
            pl.BlockSpec((None, None, n_ctx, dqk), lambda b, h: (b, h, ctx_blk, 0)),
            pl.BlockSpec((None, None, V_HEAD, n_ctx), lambda b, h: (b, h, 0, ctx_blk)),
            pl.BlockSpec(memory_space=pl.ANY),
        ],
        out_specs=pl.BlockSpec((None, n_ctx, V_HEAD), lambda b, h: (b, ctx_blk, h)),
        scratch_shapes=[pltpu.VMEM((n_ctx, n_ctx), F32)],
        input_output_aliases={3: 0},
        compiler_params=_cparams("arbitrary", "arbitrary"),
        name="mla_attention_ctx",
    )(q, k, vt, y)


def _hgrn2_tables(tt):
    n = HG_CHUNK
    t = np.arange(n)[:, None]
    r = np.arange(n)[None, :]
    fwd = [(r <= t), (r > t)]
    bwd = [(r >= t), (r < t)]
    for lvl in range(HG_LEVELS):
        m = 1 << lvl
        e = (t >> (lvl + 1)) * 2 * m + m - 1
        right = ((t >> lvl) & 1) == 1
        fwd.append(np.where(right, (r > e) & (r <= t), (r > t) & (r <= e)))
        bwd.append(np.where(right, (r > e) & (r < t), (r >= t) & (r <= e)))
    stack = lambda ms: np.tile(np.concatenate(ms, axis=0).astype(np.float32), (1, 2))
    tq = np.arange(tt)[:, None]
    ts = np.arange(tt)[None, :]
    level = np.floor(np.log2(np.maximum(tq ^ ts, 1))).astype(np.int32)
    lv_f = np.where(ts < tq, level, -1).astype(np.int32)
    return stack(fwd), stack(bwd), lv_f, np.ascontiguousarray(lv_f.T)


def _hgrn2_kernel(qf_ref, ff_ref, vf_ref, qb_ref, fb_ref, vb_ref, lb_ref, mf_ref, mb_ref, lvf_ref,
                  lvb_ref, of_ref, ob_ref, sf_ref, sb_ref, *, heads):
    n = HG_CHUNK
    tt = qf_ref.shape[0]
    wb = qf_ref.shape[1]
    nc = tt // n
    n_levels = HG_LEVELS + int(np.log2(nc))

    @pl.when(pl.program_id(2) == 0)
    def _():
        sf_ref[...] = jnp.zeros_like(sf_ref)
        sb_ref[...] = jnp.zeros_like(sb_ref)

    lb = lb_ref[...]
    row = lax.broadcasted_iota(jnp.int32, (tt, 1), 0)

    def prod(vs):
        return functools.reduce(lambda a, b: a * b, vs) if vs else None

    def by_chunk(pieces):
        return jnp.concatenate(
            [x[c * n:(c + 1) * n] if m is None else x[c * n:(c + 1) * n] * m
             for c, (x, m) in enumerate(pieces)], axis=0)

    def one_direction(q_ref, f_ref, v_ref, m_ref, lv_ref, o_ref, s_ref, backward):
        f = lb + (1.0 - lb) * _sigmoid(f_ref[...])
        g = jnp.log(f)
        g_hi = g.astype(BF16)
        g_lo = (g - g_hi.astype(F32)).astype(BF16)
        lanes = lambda x: jnp.concatenate([x[c * n:(c + 1) * n] for c in range(nc)], axis=1)
        g2 = jnp.concatenate([lanes(g_hi), lanes(g_lo)], axis=0)
        x_all = jnp.exp(_dot(m_ref[...], g2))
        q_all = _silu(q_ref[...])
        k_all = 1.0 - f
        v_all = v_ref[...]
        lv = lv_ref[...]
        outs = []
        for h in range(heads):
            cs = slice(h * HG_DK, (h + 1) * HG_DK)
            q, k, v = q_all[:, cs], k_all[:, cs], v_all[:, cs]
            vb = v.astype(BF16)

            def xblk(r):
                return jnp.concatenate(
                    [x_all[r * n:(r + 1) * n, c * wb + h * HG_DK:c * wb + (h + 1) * HG_DK]
                     for c in range(nc)], axis=0)

            eq_c = xblk(0)
            qe = q * eq_c
            ke = k * xblk(1)
            edge = (lambda c: c * n) if backward else (lambda c: c * n + n - 1)
            tot = [eq_c[edge(c):edge(c) + 1] for c in range(nc)]
            before = lambda c, lo: prod([tot[j] for j in range(lo, c)])
            after = lambda c, hi: prod([tot[j] for j in range(c + 1, hi)])

            a = jnp.zeros((tt, tt), F32)
            for lvl in range(n_levels):
                if lvl < HG_LEVELS:
                    is_q = ((row >> lvl) & 1) == (0 if backward else 1)
                    z = xblk(2 + lvl) * jnp.where(is_q, q, k)
                else:
                    mc = 1 << (lvl - HG_LEVELS)
                    pieces = []
                    for c in range(nc):
                        lo = (c // mc) * mc
                        is_left = ((c // mc) & 1) == 0
                        if is_left == backward:
                            pieces.append((qe, after(c, lo + mc) if backward else before(c, lo)))
                        else:
                            pieces.append((ke, before(c, lo) if backward else after(c, lo + mc)))
                    z = by_chunk(pieces)
                zb = z.astype(BF16)
                a = jnp.where(lv == lvl, _dot_nt(zb, zb), a)

            if backward:
                qe_t = by_chunk([(qe, after(c, nc)) for c in range(nc)])
                ke_t = by_chunk([(ke, before(c, 0)) for c in range(nc)])
            else:
                qe_t = by_chunk([(qe, before(c, 0)) for c in range(nc)])
                ke_t = by_chunk([(ke, after(c, nc)) for c in range(nc)])
            s_t = s_ref[h]
            o = _dot_nt(qe_t.astype(BF16), s_t.astype(BF16)) + _dot(a.astype(BF16), vb)
            o = o + jnp.sum(q * k, axis=-1, keepdims=True) * v
            s_ref[h] = s_t * prod(tot) + _dot_tn(vb, ke_t.astype(BF16))
            outs.append(o)
        o_ref[...] = outs[0] if heads == 1 else jnp.concatenate(outs, axis=1)

    one_direction(qf_ref, ff_ref, vf_ref, mf_ref, lvf_ref, of_ref, sf_ref, False)
    one_direction(qb_ref, fb_ref, vb_ref, mb_ref, lvb_ref, ob_ref, sb_ref, True)


def _hgrn2(p, lb, *, seq, heads_per_step=2):
    n_batch, t, n_in = p.shape
    d_k = lb.shape[-1]
    n_heads = d_k // HG_DK
    hb = heads_per_step if n_heads % heads_per_step == 0 else 1
    wb = hb * HG_DK
    tt = t - seq
    assert seq % tt == 0 and tt % HG_CHUNK == 0 and n_in == 5 * d_k
    assert (tt // HG_CHUNK) & (tt // HG_CHUNK - 1) == 0
    nt = t // tt
    ncol = d_k // wb
    mf, mb, lvf, lvb = _hgrn2_tables(tt)

    def tile_f(s):
        return jnp.where(s == 0, nt - 1, s - 1)

    def tile_b(s):
        return jnp.where(s == 0, nt - 1, nt - 1 - s)

    def col(tile, grp):
        return pl.BlockSpec((None, tt, wb), lambda b, g, s: (b, tile(s), grp * ncol + g))

    const = lambda shape: pl.BlockSpec(shape, lambda b, g, s: (0, 0))
    out_sd = jax.ShapeDtypeStruct((n_batch, t, d_k), F32)
    return pl.pallas_call(
        functools.partial(_hgrn2_kernel, heads=hb),
        out_shape=(out_sd, out_sd),
        grid=(n_batch, ncol, nt),
        in_specs=[col(tile_f, 0), col(tile_f, 1), col(tile_f, 3),
                  col(tile_b, 0), col(tile_b, 2), col(tile_b, 3),
                  pl.BlockSpec((1, wb), lambda b, g, s: (0, g)),
                  const(mf.shape), const(mb.shape), const(lvf.shape), const(lvb.shape)],
        out_specs=(pl.BlockSpec((None, tt, wb), lambda b, g, s: (b, tile_f(s), g)),
                   pl.BlockSpec((None, tt, wb), lambda b, g, s: (b, tile_b(s), g))),
        scratch_shapes=[pltpu.VMEM((hb, HG_DK, HG_DK), F32), pltpu.VMEM((hb, HG_DK, HG_DK), F32)],
        compiler_params=_cparams("arbitrary", "arbitrary", "arbitrary"),
        name="hgrn2_scan",
    )(p, p, p, p, p, p, lb.reshape(1, d_k), jnp.asarray(mf, BF16), jnp.asarray(mb, BF16),
      jnp.asarray(lvf), jnp.asarray(lvb))


def _outproj_hg_kernel(of_ref, ob_ref, gate_ref, nw_ref, w_ref, x_ref, g_ref, o_ref, y_ref,
                       *, n_batch, seq, t_total):
    tm = x_ref.shape[0]
    nw = nw_ref[...]
    dv = nw.shape[1]
    for h in range(of_ref.shape[1] // dv):
        cs = slice(h * dv, (h + 1) * dv)
        o = of_ref[:, cs] + ob_ref[:, cs]
        y = o * lax.rsqrt(jnp.mean(o * o, axis=-1, keepdims=True) + EPS) * nw
        y_ref[:, cs] = (y * _silu(gate_ref[:, cs])).astype(BF16)
    y = _dot(y_ref[...], w_ref[...])

    def finish(r0, r1, rid):
        o_ref[r0:r1, :] = x_ref[r0:r1, :] + g_ref[pl.ds(rid, 1), :] * y[r0:r1, :]

    _for_row_groups(tm, t_total, seq, n_batch, finish)


def _out_proj_hg(o_f, o_b, p, norm_w, w_out, xs, mods, layer, w_layer, *, seq):
    n_batch, t, d = xs.shape
    d_v = o_f.shape[2]
    tm = _row_tile(t) // 2
    assert tm % 16 == 0
    gate_blk = (p.shape[2] - d_v) // d_v
    kern = functools.partial(_outproj_hg_kernel, n_batch=n_batch, seq=seq, t_total=t)
    rows = lambda width, cblk: pl.BlockSpec((None, tm, width), lambda b, i: (b, i, cblk))
    return pl.pallas_call(
        kern,
        out_shape=jax.ShapeDtypeStruct((n_batch, t, d), F32),
        grid=(n_batch, t // tm),
        in_specs=[
            rows(d_v, 0), rows(d_v, 0), rows(d_v, gate_blk),
            pl.BlockSpec((1, norm_w.shape[0]), lambda b, i: (0, 0)),
            pl.BlockSpec((None, d_v, d), lambda b, i: (w_layer, 0, 0)),
            rows(d, 0),
            _mod_spec(d, 2, layer, 2),
        ],
        out_specs=rows(d, 0),
        scratch_shapes=[pltpu.VMEM((tm, d_v), BF16)],
        compiler_params=_cparams("arbitrary", "arbitrary"),
        name="out_proj_hgrn2",
    )(o_f, o_b, p, norm_w.reshape(1, -1), w_out, xs, mods)


def _final_norm_kernel(x_ref, w_ref, o_ref):
    x = x_ref[...]
    o_ref[...] = x * lax.rsqrt(jnp.mean(x * x, axis=-1, keepdims=True) + EPS) * w_ref[...]


def _final_norm(xs, w, *, seq):
    n_batch, t, d = xs.shape
    tm = t - seq
    return pl.pallas_call(
        _final_norm_kernel,
        out_shape=jax.ShapeDtypeStruct((n_batch, seq, d), F32),
        grid=(n_batch, seq // tm),
        in_specs=[pl.BlockSpec((None, tm, d), lambda b, i: (b, i, 0)),
                  pl.BlockSpec((1, d), lambda b, i: (0, 0))],
        out_specs=pl.BlockSpec((None, tm, d), lambda b, i: (b, i, 0)),
        compiler_params=_cparams("arbitrary", "arbitrary"),
        name="final_norm",
    )(xs, w.reshape(1, d))


def _rot_columns(w):
    q = QK_ROPE // 4
    return jnp.concatenate([-w[..., q:2 * q], w[..., :q], -w[..., 3 * q:], w[..., 2 * q:3 * q]], axis=-1)


def _rope_tables(seq, n_ctx):
    t = np.arange(seq)
    row, colm = t // GRID_W, t % GRID_W
    half = QK_ROPE // 2
    inv_freq = ROPE_BASE ** (-np.arange(0, half, 2, dtype=np.float32) / half)
    ang_r = row.astype(np.float32)[:, None] * inv_freq
    ang_c = colm.astype(np.float32)[:, None] * inv_freq
    ang = jnp.asarray(np.concatenate([ang_r, ang_r, ang_c, ang_c], axis=-1), F32)
    cos = jnp.concatenate([jnp.cos(ang), jnp.ones((n_ctx, QK_ROPE), F32)], axis=0)
    sin = jnp.concatenate([jnp.sin(ang), jnp.zeros((n_ctx, QK_ROPE), F32)], axis=0)
    return cos, sin


def kernel(x, c, ctx, c_ctx, ada_w, ada_b, norm_mix_w, norm_mlp_w, ab_w_in, ab_w_out, lru_conv_w, lru_conv_b, lru_w_a, lru_b_a, lru_w_x, lru_b_x, lru_lambda, mla_q_norm_w, mla_w_uq, mla_kv_norm_w, mla_w_ukv, hg_w_in, hg_lb_logits, hg_norm_w, hg_w_out, mlp_w1, mlp_w2, final_norm_w):
    n_batch, seq, d = x.shape
    n_ctx = ctx.shape[1]
    depth = ada_w.shape[0]
    assert n_batch + 1 <= SUBLANES

    xs = jnp.concatenate([x, ctx], axis=1)
    cond8 = jnp.zeros((SUBLANES, d), F32).at[:n_batch].set(c).at[n_batch].set(c_ctx)
    mods = _ada_all(cond8, ada_w, ada_b)
    lb_all = _lower_bounds(hg_lb_logits)
    cos_t, sin_t = _rope_tables(seq, n_ctx)

    d_lru = lru_conv_w.shape[-1]
    q_lora = mla_q_norm_w.shape[-1]
    col_kr = ab_w_in.shape[2] - QK_ROPE
    ab_w_in_ext = jnp.concatenate([ab_w_in, _rot_columns(ab_w_in[..., col_kr:])], axis=-1).astype(BF16)
    n_ab = ab_w_in_ext.shape[2]
    tn_ab = n_ab // 5 if n_ab % (5 * LANES) == 0 else n_ab
    hg_w_in_b = hg_w_in.astype(BF16)
    n_hg = hg_w_in_b.shape[2]
    tn_hg = 1024 if n_hg % 1024 == 0 else n_hg // 5
    ab_w_out_b = ab_w_out.astype(BF16)
    hg_w_out_b = hg_w_out.astype(BF16)
    w1_b = mlp_w1.astype(BF16)
    w2_b = mlp_w2.astype(BF16)
    wq = mla_w_uq.reshape(-1, q_lora, MLA_HEADS, QK_NOPE + QK_ROPE)
    wq_ext = jnp.concatenate([wq, _rot_columns(wq[..., QK_NOPE:])], axis=-1)
    wq_ext = wq_ext.reshape(wq.shape[0], q_lora, -1).astype(BF16)
    wkv_b = mla_w_ukv.astype(BF16)
    lru_w_a_b = lru_w_a.astype(BF16)
    lru_w_x_b = lru_w_x.astype(BF16)

    for l in range(depth):
        if l % 2 == 0:
            e = l // 2
            p = _in_proj(xs, norm_mix_w[l], mods, ab_w_in_ext, l, e, seq=seq, tn=tn_ab)
            ya = _lru(p, lru_conv_w[e], lru_conv_b[e], lru_w_a_b[e], lru_b_a[e],
                      lru_w_x_b[e], lru_b_x[e], lru_lambda[e], seq=seq)
            q, k, vt = _mla_qkv(p, mla_q_norm_w[e], mla_kv_norm_w[e], wq_ext[e], wkv_b[e],
                                cos_t, sin_t, seq=seq, col0=2 * d_lru)
            yb = _attention(q, k, vt, seq=seq)
            xs = _out_proj_ab(ya, yb, ab_w_out_b, xs, mods, l, e, seq=seq)
        else:
            o = l // 2
            p = _in_proj(xs, norm_mix_w[l], mods, hg_w_in_b, l, o, seq=seq, tn=tn_hg)
            o_f, o_b = _hgrn2(p, lb_all[l], seq=seq)
            xs = _out_proj_hg(o_f, o_b, p, hg_norm_w[o], hg_w_out_b, xs, mods, l, o, seq=seq)
        xs = _mlp(xs, norm_mlp_w[l], mods, w1_b, w2_b, l, seq=seq)
    return _final_norm(xs, final_norm_w, seq=seq)
```

```python
import functools

import numpy as np
import jax
import jax.numpy as jnp
from jax import lax
from jax.experimental import pallas as pl
from jax.experimental.pallas import tpu as pltpu

GRID_W = 64
EPS = 1e-6
N_MOD = 6
LRU_HEADS = 8
CONV_W = 4
CONV_LEFT = CONV_W // 2
LRU_C = 8.0
MLA_HEADS = 8
QK_NOPE = 128
QK_ROPE = 64
V_HEAD = 128
ROPE_BASE = 10000.0
HG_DK = 128
HG_CHUNK = 64

LANES = 128
SUBLANES = 8
VMEM_LIMIT_BYTES = 56 * 1024 * 1024
NORM_ROW_CHUNK = 32
ATTN_KEY_CHUNK = 1024
ATTN_Q_TILE = 512

F32 = jnp.float32
BF16 = jnp.bfloat16

HG_LEVELS = int(np.log2(HG_CHUNK))


def _cparams(*sem):
    return pltpu.CompilerParams(dimension_semantics=sem, vmem_limit_bytes=VMEM_LIMIT_BYTES)


def _dot(a, b):
    return jnp.dot(a, b, preferred_element_type=F32)


def _dot_nt(a, b):
    return lax.dot_general(a, b, (((1,), (1,)), ((), ())), preferred_element_type=F32)


def _dot_tn(a, b):
    return lax.dot_general(a, b, (((0,), (0,)), ((), ())), preferred_element_type=F32)


def _sigmoid(x):
    return 1.0 / (1.0 + jnp.exp(-x))


def _silu(x):
    return x * _sigmoid(x)


def _gelu_tanh(x):
    c = np.float32(np.sqrt(2.0 / np.pi))
    return 0.5 * x * (1.0 + jnp.tanh(c * (x + np.float32(0.044715) * (x * x * x))))


def _softplus(x):
    return jnp.maximum(x, 0.0) + jnp.log1p(jnp.exp(-jnp.abs(x)))


def _row_tile(t_total):
    tm = t_total // 8
    assert tm * 8 == t_total and tm % 16 == 0, t_total
    return tm


def _ada_kernel(c_ref, w_ref, b_ref, o_ref):
    a = _silu(c_ref[...]).astype(BF16)
    o_ref[...] = _dot(a, w_ref[...].astype(BF16)) + b_ref[...]


def _ada_all(cond8, ada_w, ada_b):
    n_layers, d, n6 = ada_w.shape
    tn = 1024 if n6 % 1024 == 0 else n6 // N_MOD
    return pl.pallas_call(
        _ada_kernel,
        out_shape=jax.ShapeDtypeStruct((n_layers, SUBLANES, n6), F32),
        grid=(n_layers, n6 // tn),
        in_specs=[
            pl.BlockSpec((SUBLANES, d), lambda l, j: (0, 0)),
            pl.BlockSpec((None, d, tn), lambda l, j: (l, 0, j)),
            pl.BlockSpec((None, 1, tn), lambda l, j: (l, 0, j)),
        ],
        out_specs=pl.BlockSpec((None, SUBLANES, tn), lambda l, j: (l, 0, j)),
        compiler_params=_cparams("arbitrary", "arbitrary"),
        name="ada_params",
    )(cond8, ada_w, ada_b.reshape(n_layers, 1, n6))


def _lb_kernel(x_ref, o_ref):
    n = x_ref.shape[0]
    rows = [x_ref[pl.ds(l, 1), :] for l in range(n)]
    mx = functools.reduce(jnp.maximum, rows)
    es = [jnp.exp(r - mx) for r in rows]
    tot = functools.reduce(lambda a, b: a + b, es)
    ps = [e / tot for e in es]
    acc = ps[0]
    o_ref[pl.ds(0, 1), :] = acc - ps[0]
    for l in range(1, n):
        acc = acc + ps[l]
        o_ref[pl.ds(l, 1), :] = acc - ps[0]


def _lower_bounds(logits):
    return pl.pallas_call(
        _lb_kernel, out_shape=jax.ShapeDtypeStruct(logits.shape, F32), name="hgrn2_lower_bounds"
    )(logits.astype(F32))


def _for_row_groups(tm, t_total, seq, n_batch, fn):
    b = pl.program_id(0)
    i = pl.program_id(1)
    i_mixed, split = divmod(seq, tm)
    if i_mixed > 0:
        pl.when(i < i_mixed)(lambda: fn(0, tm, b))
    if i_mixed < t_total // tm:
        @pl.when(i == i_mixed)
        def _():
            if split > 0:
                fn(0, split, b)
            fn(split, tm, n_batch)
    if i_mixed + 1 < t_total // tm:
        pl.when(i > i_mixed)(lambda: fn(0, tm, n_batch))


def _norm_modulate_rows(x_ref, nw_ref, sh_ref, sc_ref, h_ref, r0, r1, rid):
    gain = nw_ref[...] * (1.0 + sc_ref[pl.ds(rid, 1), :])
    shift = sh_ref[pl.ds(rid, 1), :]
    rc = NORM_ROW_CHUNK
    for r in range(r0, r1, rc):
        x = x_ref[r:min(r + rc, r1), :]
        ms = jnp.mean(x * x, axis=-1, keepdims=True)
        h_ref[r:min(r + rc, r1), :] = (x * lax.rsqrt(ms + EPS) * gain + shift).astype(h_ref.dtype)


def _mod_spec(d, k, layer, nargs):
    if nargs == 2:
        return pl.BlockSpec((None, SUBLANES, d), lambda b, i: (layer, 0, k))
    return pl.BlockSpec((None, SUBLANES, d), lambda b, i, j: (layer, 0, k))


def _inproj_kernel(x_ref, nw_ref, sh_ref, sc_ref, w_ref, *rest, n_batch, seq, t_total, n_lo_tiles):
    out_refs, h_ref = rest[:-1], rest[-1]
    tm = x_ref.shape[0]
    j = pl.program_id(2)

    @pl.when(j == 0)
    def _():
        _for_row_groups(tm, t_total, seq, n_batch, functools.partial(
            _norm_modulate_rows, x_ref, nw_ref, sh_ref, sc_ref, h_ref))

    y = _dot(h_ref[...], w_ref[...])
    if len(out_refs) == 1:
        out_refs[0][...] = y.astype(out_refs[0].dtype)
    else:
        lo_ref, hi_ref = out_refs

        @pl.when(j < n_lo_tiles)
        def _():
            lo_ref[...] = y.astype(lo_ref.dtype)

        @pl.when(j >= n_lo_tiles)
        def _():
            hi_ref[...] = y.astype(hi_ref.dtype)


def _in_proj(xs, norm_w, mods, w, layer, w_layer, *, seq, tn, n_f32_cols=0):
    n_batch, t, d = xs.shape
    n_out = w.shape[2]
    tm = _row_tile(t) * 2
    n_lo = n_out - n_f32_cols
    assert n_out % tn == 0 and n_lo % tn == 0 and n_lo > 0
    n_lo_tiles = n_lo // tn
    kern = functools.partial(_inproj_kernel, n_batch=n_batch, seq=seq, t_total=t, n_lo_tiles=n_lo_tiles)
    out_shape = [jax.ShapeDtypeStruct((n_batch, t, n_lo), BF16)]
    out_specs = [pl.BlockSpec((None, tm, tn), lambda b, i, j: (b, i, jnp.minimum(j, n_lo_tiles - 1)))]
    if n_f32_cols:
        out_shape.append(jax.ShapeDtypeStruct((n_batch, t, n_f32_cols), F32))
        out_specs.append(
            pl.BlockSpec((None, tm, tn), lambda b, i, j: (b, i, jnp.maximum(j - n_lo_tiles, 0))))
    outs = pl.pallas_call(
        kern,
        out_shape=out_shape,
        grid=(n_batch, t // tm, n_out // tn),
        in_specs=[
            pl.BlockSpec((None, tm, d), lambda b, i, j: (b, i, 0)),
            pl.BlockSpec((1, d), lambda b, i, j: (0, 0)),
            _mod_spec(d, 0, layer, 3),
            _mod_spec(d, 1, layer, 3),
            pl.BlockSpec((None, d, tn), lambda b, i, j: (w_layer, 0, j)),
        ],
        out_specs=out_specs,
        scratch_shapes=[pltpu.VMEM((tm, d), BF16)],
        compiler_params=_cparams("arbitrary", "arbitrary", "arbitrary"),
        name="in_proj",
    )(xs, norm_w.reshape(1, d), mods, mods, w)
    return outs if n_f32_cols else outs[0]


def _mlp_kernel(x_ref, nw_ref, sh_ref, sc_ref, g_ref, w1_ref, w2_ref, o_ref, h_ref, acc_ref,
                *, n_batch, seq, t_total):
    f = pl.program_id(2)
    tm = x_ref.shape[0]

    @pl.when(f == 0)
    def _():
        _for_row_groups(tm, t_total, seq, n_batch, functools.partial(
            _norm_modulate_rows, x_ref, nw_ref, sh_ref, sc_ref, h_ref))
        acc_ref[...] = jnp.zeros_like(acc_ref)

    a = jnp.maximum(_dot(h_ref[...], w1_ref[...]), 0.0)
    acc_ref[...] += _dot((a * a).astype(BF16), w2_ref[...])

    @pl.when(f == pl.num_programs(2) - 1)
    def _():
        def finish(r0, r1, rid):
            o_ref[r0:r1, :] = x_ref[r0:r1, :] + g_ref[pl.ds(rid, 1), :] * acc_ref[r0:r1, :]

        _for_row_groups(tm, t_total, seq, n_batch, finish)


def _mlp(xs, norm_w, mods, w1, w2, layer, *, seq):
    n_batch, t, d = xs.shape
    d_ff = w1.shape[2]
    tm = _row_tile(t)
    tf = 1024 if d_ff % 1024 == 0 else d_ff
    kern = functools.partial(_mlp_kernel, n_batch=n_batch, seq=seq, t_total=t)
    return pl.pallas_call(
        kern,
        out_shape=jax.ShapeDtypeStruct((n_batch, t, d), F32),
        grid=(n_batch, t // tm, d_ff // tf),
        in_specs=[
            pl.BlockSpec((None, tm, d), lambda b, i, f: (b, i, 0)),
            pl.BlockSpec((1, d), lambda b, i, f: (0, 0)),
            _mod_spec(d, 3, layer, 3),
            _mod_spec(d, 4, layer, 3),
            _mod_spec(d, 5, layer, 3),
            pl.BlockSpec((None, d, tf), lambda b, i, f: (layer, 0, f)),
            pl.BlockSpec((None, tf, d), lambda b, i, f: (layer, f, 0)),
        ],
        out_specs=pl.BlockSpec((None, tm, d), lambda b, i, f: (b, i, 0)),
        scratch_shapes=[pltpu.VMEM((tm, d), BF16), pltpu.VMEM((tm, d), F32)],
        compiler_params=_cparams("arbitrary", "arbitrary", "arbitrary"),
        name="mlp",
    )(xs, norm_w.reshape(1, d), mods, mods, mods, w1, w2)


def _outproj_ab_kernel(ya_ref, yb_ref, w_ref, x_ref, g_ref, o_ref, *, n_batch, seq, t_total):
    tm = x_ref.shape[0]
    da = ya_ref.shape[1]
    y = _dot(ya_ref[...], w_ref[0:da, :]) + _dot(yb_ref[...], w_ref[da:, :])

    def finish(r0, r1, rid):
        o_ref[r0:r1, :] = x_ref[r0:r1, :] + g_ref[pl.ds(rid, 1), :] * y[r0:r1, :]

    _for_row_groups(tm, t_total, seq, n_batch, finish)


def _out_proj_ab(ya, yb, w_out, xs, mods, layer, w_layer, *, seq):
    n_batch, t, d = xs.shape
    da, db = ya.shape[2], yb.shape[2]
    tm = _row_tile(t)
    kern = functools.partial(_outproj_ab_kernel, n_batch=n_batch, seq=seq, t_total=t)
    return pl.pallas_call(
        kern,
        out_shape=jax.ShapeDtypeStruct((n_batch, t, d), F32),
        grid=(n_batch, t // tm),
        in_specs=[
            pl.BlockSpec((None, tm, da), lambda b, i: (b, i, 0)),
            pl.BlockSpec((None, tm, db), lambda b, i: (b, i, 0)),
            pl.BlockSpec((None, da + db, d), lambda b, i: (w_layer, 0, 0)),
            pl.BlockSpec((None, tm, d), lambda b, i: (b, i, 0)),
            _mod_spec(d, 2, layer, 2),
        ],
        out_specs=pl.BlockSpec((None, tm, d), lambda b, i: (b, i, 0)),
        compiler_params=_cparams("arbitrary", "arbitrary"),
        name="out_proj_ab",
    )(ya, yb, w_out, xs, mods)


def _lru_kernel(gate_ref, u_ref, cw_ref, cb_ref, wa_ref, ba_ref, wx_ref, bx_ref, lam_ref, o_ref,
                upad_ref, af_ref, bf_ref, ab_ref, bb_ref, hf_ref, hb_ref, *, seq, chunk):
    t_total = u_ref.shape[0]
    n_ctx = t_total - seq
    pad = SUBLANES
    cw = cw_ref[...]
    cb = cb_ref[...]
    sp = _softplus(-lam_ref[...])
    zeros_pad = jnp.zeros((pad, LANES), F32)

    def gates(uc, d):
        ucb = uc.astype(BF16)
        r = _sigmoid(_dot(ucb, wa_ref[d]) + ba_ref[pl.ds(d, 1), :])
        i = _sigmoid(_dot(ucb, wx_ref[d]) + bx_ref[pl.ds(d, 1), :])
        log_a = (-LRU_C) * r * sp[d:d + 1, :]
        a = jnp.exp(log_a)
        return a, jnp.sqrt(1.0 - a * a) * (i * uc)

    def prepare(src0, dst0, n):
        upad_ref[pl.ds(0, pad), :] = zeros_pad
        upad_ref[pl.ds(pad + n, pad), :] = zeros_pad
        upad_ref[pl.ds(pad, n), :] = u_ref[pl.ds(src0, n), :].astype(F32)
        ch = min(chunk, n)

        def body(c, carry):
            r0 = pl.multiple_of(c * ch, SUBLANES)
            win = upad_ref[pl.ds(r0, ch + 2 * pad), :]
            uc = cb
            for j in range(CONV_W):
                off = pad + j - CONV_LEFT
                uc = uc + win[off:off + ch, :] * cw[j:j + 1, :]
            a0, b0 = gates(uc, 0)
            a1, b1 = gates(uc, 1)
            d0 = pl.multiple_of(dst0 + r0, SUBLANES)
            af_ref[pl.ds(d0, ch), :] = a0
            bf_ref[pl.ds(d0, ch), :] = b0
            ab_ref[pl.ds(d0, ch), :] = a1
            bb_ref[pl.ds(d0, ch), :] = b1
            return carry

        lax.fori_loop(0, n // ch, body, 0)

    prepare(seq, 0, n_ctx)
    prepare(0, n_ctx, seq)

    row = lax.broadcasted_iota(jnp.int32, (SUBLANES, LANES), 0)

    def scan_fwd(a, b):
        for s in (1, 2, 4):
            m = row >= s
            a_s = pltpu.roll(a, s, 0)
            b_s = pltpu.roll(b, s, 0)
            b = jnp.where(m, a * b_s + b, b)
            a = jnp.where(m, a * a_s, a)
        return a, b

    def scan_bwd(a, b):
        for s in (1, 2, 4):
            m = row < SUBLANES - s
            a_s = pltpu.roll(a, SUBLANES - s, 0)
            b_s = pltpu.roll(b, SUBLANES - s, 0)
            b = jnp.where(m, a * b_s + b, b)
            a = jnp.where(m, a * a_s, a)
        return a, b

    def scan_range(g0, n_groups, carry):
        def body(g, c):
            cf, cbk = c
            rf = pl.multiple_of((g0 + g) * SUBLANES, SUBLANES)
            a, b = scan_fwd(af_ref[pl.ds(rf, SUBLANES), :], bf_ref[pl.ds(rf, SUBLANES), :])
            hf = a * cf + b
            hf_ref[pl.ds(rf, SUBLANES), :] = hf
            rb = pl.multiple_of((g0 + n_groups - 1 - g) * SUBLANES, SUBLANES)
            a, b = scan_bwd(ab_ref[pl.ds(rb, SUBLANES), :], bb_ref[pl.ds(rb, SUBLANES), :])
            hb = a * cbk + b
            hb_ref[pl.ds(rb, SUBLANES), :] = hb
            cf = jnp.broadcast_to(hf[SUBLANES - 1:SUBLANES, :], (SUBLANES, LANES))
            cbk = jnp.broadcast_to(hb[0:1, :], (SUBLANES, LANES))
            return cf, cbk

        return lax.fori_loop(0, n_groups, body, carry, unroll=4)

    zero = jnp.zeros((SUBLANES, LANES), F32)
    carry = scan_range(0, n_ctx // SUBLANES, (zero, zero))
    scan_range(n_ctx // SUBLANES, seq // SUBLANES, carry)

    def finish(src0, dst0, n):
        ch = min(chunk, n)

        def body(c, carry):
            r0 = c * ch
            s0 = pl.multiple_of(src0 + r0, SUBLANES)
            d0 = pl.multiple_of(dst0 + r0, SUBLANES)
            h = hf_ref[pl.ds(s0, ch), :] + hb_ref[pl.ds(s0, ch), :]
            gate = gate_ref[pl.ds(d0, ch), :].astype(F32)
            o_ref[pl.ds(d0, ch), :] = (h * _gelu_tanh(gate)).astype(o_ref.dtype)
            return carry

        lax.fori_loop(0, n // ch, body, 0)

    finish(0, seq, n_ctx)
    finish(n_ctx, 0, seq)


def _lru(p, conv_w, conv_b, w_a, b_a, w_x, b_x, lam, *, seq):
    n_batch, t, _ = p.shape
    d_lru = conv_w.shape[1]
    n_heads = w_a.shape[1]
    assert d_lru // n_heads == LANES
    n_ctx = t - seq
    chunk = 256
    assert seq % min(chunk, seq) == 0 and n_ctx % min(chunk, n_ctx) == 0
    kern = functools.partial(_lru_kernel, seq=seq, chunk=chunk)
    vec = lambda rows: pl.BlockSpec((rows, LANES), lambda b, h: (0, h))
    wspec = pl.BlockSpec((2, None, LANES, LANES), lambda b, h: (0, h, 0, 0))
    scratch_rows = pltpu.VMEM((t, LANES), F32)
    return pl.pallas_call(
        kern,
        out_shape=jax.ShapeDtypeStruct((n_batch, t, d_lru), BF16),
        grid=(n_batch, n_heads),
        in_specs=[
            pl.BlockSpec((None, t, LANES), lambda b, h: (b, 0, h)),
            pl.BlockSpec((None, t, LANES), lambda b, h: (b, 0, n_heads + h)),
            vec(CONV_W), vec(1), wspec, vec(2), wspec, vec(2), vec(2),
        ],
        out_specs=pl.BlockSpec((None, t, LANES), lambda b, h: (b, 0, h)),
        scratch_shapes=[pltpu.VMEM((max(seq, n_ctx) + 2 * SUBLANES, LANES), F32),
                        scratch_rows, scratch_rows, scratch_rows, scratch_rows,
                        scratch_rows, scratch_rows],
        compiler_params=_cparams("arbitrary", "arbitrary"),
        name="rglru",
    )(p, p, conv_w, conv_b.reshape(1, d_lru), w_a, b_a, w_x, b_x, lam)


def _mla_qkv_kernel(cq_ref, ckv_ref, kr_ref, qn_ref, kvn_ref, wq_ref, wkv_ref, cos_ref, sin_ref,
                    q_ref, k_ref, v_ref, *, scale):
    def rms(x, w):
        return (x * lax.rsqrt(jnp.mean(x * x, axis=-1, keepdims=True) + EPS) * w).astype(BF16)

    cos = cos_ref[...]
    sin = sin_ref[...]
    q_all = _dot(rms(cq_ref[...].astype(F32), qn_ref[...]), wq_ref[...])
    kv_all = _dot(rms(ckv_ref[...].astype(F32), kvn_ref[...]), wkv_ref[...])
    kr = kr_ref[...].astype(F32)
    kr_rot = kr[:, :QK_ROPE] * cos + kr[:, QK_ROPE:] * sin
    wq_head = QK_NOPE + 2 * QK_ROPE
    wkv_head = QK_NOPE + V_HEAD
    for h in range(q_ref.shape[0]):
        qh = q_all[:, h * wq_head:(h + 1) * wq_head]
        qr = qh[:, QK_NOPE:QK_NOPE + QK_ROPE] * cos + qh[:, QK_NOPE + QK_ROPE:] * sin
        q_ref[h, :, 0:QK_NOPE] = (qh[:, :QK_NOPE] * scale).astype(BF16)
        q_ref[h, :, QK_NOPE:] = (qr * scale).astype(BF16)
        kvh = kv_all[:, h * wkv_head:(h + 1) * wkv_head]
        k_ref[h, :, 0:QK_NOPE] = kvh[:, :QK_NOPE].astype(BF16)
        k_ref[h, :, QK_NOPE:] = kr_rot.astype(BF16)
        v_ref[h] = kvh[:, QK_NOPE:].T.astype(BF16)


def _mla_qkv(p, q_norm_w, kv_norm_w, wq_ext, wkv, cos_t, sin_t, *, seq, col0):
    n_batch, t, _ = p.shape
    q_lora = q_norm_w.shape[0]
    kv_lora = kv_norm_w.shape[0]
    assert q_lora == kv_lora and col0 % q_lora == 0 and (col0 + 2 * q_lora) % LANES == 0
    n_ctx = t - seq
    tm = n_ctx
    nh = MLA_HEADS
    dqk = QK_NOPE + QK_ROPE
    scale = np.float32(dqk ** -0.5 * np.log2(np.e))
    cq_blk = col0 // q_lora
    kr_blk = (col0 + 2 * q_lora) // LANES
    full = lambda shape: pl.BlockSpec(shape, lambda b, i: (0,) * len(shape))
    return pl.pallas_call(
        functools.partial(_mla_qkv_kernel, scale=scale),
        out_shape=(jax.ShapeDtypeStruct((n_batch, nh, t, dqk), BF16),
                   jax.ShapeDtypeStruct((n_batch, nh, t, dqk), BF16),
                   jax.ShapeDtypeStruct((n_batch, nh, V_HEAD, t), BF16)),
        grid=(n_batch, t // tm),
        in_specs=[
            pl.BlockSpec((None, tm, q_lora), lambda b, i: (b, i, cq_blk)),
            pl.BlockSpec((None, tm, kv_lora), lambda b, i: (b, i, cq_blk + 1)),
            pl.BlockSpec((None, tm, LANES), lambda b, i: (b, i, kr_blk)),
            full((1, q_lora)), full((1, kv_lora)), full(wq_ext.shape), full(wkv.shape),
            pl.BlockSpec((tm, QK_ROPE), lambda b, i: (i, 0)),
            pl.BlockSpec((tm, QK_ROPE), lambda b, i: (i, 0)),
        ],
        out_specs=(pl.BlockSpec((None, nh, tm, dqk), lambda b, i: (b, 0, i, 0)),
                   pl.BlockSpec((None, nh, tm, dqk), lambda b, i: (b, 0, i, 0)),
                   pl.BlockSpec((None, nh, V_HEAD, tm), lambda b, i: (b, 0, 0, i))),
        compiler_params=_cparams("arbitrary", "arbitrary"),
        name="mla_qkv",
    )(p, p, p, q_norm_w.reshape(1, q_lora), kv_norm_w.reshape(1, kv_lora), wq_ext, wkv, cos_t, sin_t)


def _softmax_values(q, k_ref, vt_ref, s_ref):
    t = k_ref.shape[0]
    bounds = list(range(0, t, ATTN_KEY_CHUNK)) + [t]
    chunks = list(zip(bounds[:-1], bounds[1:]))
    m = None
    for a, b in chunks:
        s = _dot_nt(k_ref[a:b, :], q)
        s_ref[a:b, :] = s
        cm = jnp.max(s, axis=0, keepdims=True)
        m = cm if m is None else jnp.maximum(m, cm)
    l = jnp.zeros_like(m)
    o_t = jnp.zeros((vt_ref.shape[0], q.shape[0]), F32)
    for a, b in chunks:
        p = jnp.exp2(s_ref[a:b, :] - m)
        l = l + jnp.sum(p, axis=0, keepdims=True)
        o_t = o_t + _dot(vt_ref[:, a:b], p.astype(BF16))
    return (o_t * (1.0 / l)).T


def _attn_kernel(q_ref, k_ref, vt_ref, o_ref, s_ref):
    o_ref[...] = _softmax_values(q_ref[...], k_ref, vt_ref, s_ref).astype(o_ref.dtype)


def _attention(q, k, vt, *, seq):
    n_batch, nh, t, dqk = q.shape
    n_ctx = t - seq
    tq = ATTN_Q_TILE if seq % ATTN_Q_TILE == 0 else n_ctx
    assert seq % tq == 0 and seq % n_ctx == 0
    y_lat = pl.pallas_call(
        _attn_kernel,
        out_shape=jax.ShapeDtypeStruct((n_batch, seq, nh * V_HEAD), BF16),
        grid=(n_batch, nh, seq // tq),
        in_specs=[
            pl.BlockSpec((None, None, tq, dqk), lambda b, h, i: (b, h, i, 0)),
            pl.BlockSpec((None, None, t, dqk), lambda b, h, i: (b, h, 0, 0)),
            pl.BlockSpec((None, None, V_HEAD, t), lambda b, h, i: (b, h, 0, 0)),
        ],
        out_specs=pl.BlockSpec((None, tq, V_HEAD), lambda b, h, i: (b, i, h)),
        scratch_shapes=[pltpu.VMEM((t, tq), F32)],
        compiler_params=_cparams("arbitrary", "arbitrary", "arbitrary"),
        name="mla_attention",
    )(q, k, vt)
    ctx_blk = seq // n_ctx
    y_ctx = pl.pallas_call(
        _attn_kernel,
        out_shape=jax.ShapeDtypeStruct((n_batch, n_ctx, nh * V_HEAD), BF16),
        grid=(n_batch, nh),
        in_specs=[
            pl.BlockSpec((None, None, n_ctx, dqk), lambda b, h: (b, h, ctx_blk, 0)),
            pl.BlockSpec((None, None, n_ctx, dqk), lambda b, h: (b, h, ctx_blk, 0)),
            pl.BlockSpec((None, None, V_HEAD, n_ctx), lambda b, h: (b, h, 0, ctx_blk)),
        ],
        out_specs=pl.BlockSpec((None, n_ctx, V_HEAD), lambda b, h: (b, 0, h)),
        scratch_shapes=[pltpu.VMEM((n_ctx, n_ctx), F32)],
        compiler_params=_cparams("arbitrary", "arbitrary"),
        name="mla_attention_ctx",
    )(q, k, vt)
    return jnp.concatenate([y_lat, y_ctx], axis=1)


def _hgrn2_tables(tt):
    n = HG_CHUNK
    t = np.arange(n)[:, None]
    r = np.arange(n)[None, :]
    fwd = [(r <= t), (r > t)]
    bwd = [(r >= t), (r < t)]
    for lvl in range(HG_LEVELS):
        m = 1 << lvl
        e = (t >> (lvl + 1)) * 2 * m + m - 1
        right = ((t >> lvl) & 1) == 1
        fwd.append(np.where(right, (r > e) & (r <= t), (r > t) & (r <= e)))
        bwd.append(np.where(right, (r > e) & (r < t), (r >= t) & (r <= e)))
    stack = lambda ms: np.tile(np.concatenate(ms, axis=0).astype(np.float32), (1, 2))
    tq = np.arange(tt)[:, None]
    ts = np.arange(tt)[None, :]
    level = np.floor(np.log2(np.maximum(tq ^ ts, 1))).astype(np.int32)
    lv_f = np.where(ts < tq, level, -1).astype(np.int32)
    return stack(fwd), stack(bwd), lv_f, np.ascontiguousarray(lv_f.T)


def _hgrn2_kernel(qf_ref, ff_ref, vf_ref, qb_ref, fb_ref, vb_ref, lb_ref, mf_ref, mb_ref, lvf_ref,
                  lvb_ref, of_ref, ob_ref, sf_ref, sb_ref, *, heads):
    n = HG_CHUNK
    tt = qf_ref.shape[0]
    wb = qf_ref.shape[1]
    nc = tt // n
    n_levels = HG_LEVELS + int(np.log2(nc))

    @pl.when(pl.program_id(2) == 0)
    def _():
        sf_ref[...] = jnp.zeros_like(sf_ref)
        sb_ref[...] = jnp.zeros_like(sb_ref)

    lb = lb_ref[...]
    row = lax.broadcasted_iota(jnp.int32, (tt, 1), 0)

    def prod(vs):
        return functools.reduce(lambda a, b: a * b, vs) if vs else None

    def by_chunk(pieces):
        return jnp.concatenate(
            [x[c * n:(c + 1) * n] if m is None else x[c * n:(c + 1) * n] * m
             for c, (x, m) in enumerate(pieces)], axis=0)

    def one_direction(q_ref, f_ref, v_ref, m_ref, lv_ref, o_ref, s_ref, backward):
        f = lb + (1.0 - lb) * _sigmoid(f_ref[...])
        g = jnp.log(f)
        g_hi = g.astype(BF16)
        g_lo = (g - g_hi.astype(F32)).astype(BF16)
        lanes = lambda x: jnp.concatenate([x[c * n:(c + 1) * n] for c in range(nc)], axis=1)
        g2 = jnp.concatenate([lanes(g_hi), lanes(g_lo)], axis=0)
        x_all = jnp.exp(_dot(m_ref[...], g2))
        q_all = _silu(q_ref[...].astype(F32))
        k_all = 1.0 - f
        v_all = v_ref[...].astype(F32)
        lv = lv_ref[...]
        outs = []
        for h in range(heads):
            cs = slice(h * HG_DK, (h + 1) * HG_DK)
            q, k, v = q_all[:, cs], k_all[:, cs], v_all[:, cs]
            vb = v.astype(BF16)

            def xblk(r):
                return jnp.concatenate(
                    [x_all[r * n:(r + 1) * n, c * wb + h * HG_DK:c * wb + (h + 1) * HG_DK]
                     for c in range(nc)], axis=0)

            eq_c = xblk(0)
            qe = q * eq_c
            ke = k * xblk(1)
            edge = (lambda c: c * n) if backward else (lambda c: c * n + n - 1)
            tot = [eq_c[edge(c):edge(c) + 1] for c in range(nc)]
            before = lambda c, lo: prod([tot[j] for j in range(lo, c)])
            after = lambda c, hi: prod([tot[j] for j in range(c + 1, hi)])

            a = jnp.zeros((tt, tt), F32)
            for lvl in range(n_levels):
                if lvl < HG_LEVELS:
                    is_q = ((row >> lvl) & 1) == (0 if backward else 1)
                    z = xblk(2 + lvl) * jnp.where(is_q, q, k)
                else:
                    mc = 1 << (lvl - HG_LEVELS)
                    pieces = []
                    for c in range(nc):
                        lo = (c // mc) * mc
                        is_left = ((c // mc) & 1) == 0
                        if is_left == backward:
                            pieces.append((qe, after(c, lo + mc) if backward else before(c, lo)))
                        else:
                            pieces.append((ke, before(c, lo) if backward else after(c, lo + mc)))
                    z = by_chunk(pieces)
                zb = z.astype(BF16)
                a = jnp.where(lv == lvl, _dot_nt(zb, zb), a)

            if backward:
                qe_t = by_chunk([(qe, after(c, nc)) for c in range(nc)])
                ke_t = by_chunk([(ke, before(c, 0)) for c in range(nc)])
            else:
                qe_t = by_chunk([(qe, before(c, 0)) for c in range(nc)])
                ke_t = by_chunk([(ke, after(c, nc)) for c in range(nc)])
            s_t = s_ref[h]
            o = _dot_nt(qe_t.astype(BF16), s_t.astype(BF16)) + _dot(a.astype(BF16), vb)
            o = o + jnp.sum(q * k, axis=-1, keepdims=True) * v
            s_ref[h] = s_t * prod(tot) + _dot_tn(vb, ke_t.astype(BF16))
            outs.append(o)
        o_ref[...] = outs[0] if heads == 1 else jnp.concatenate(outs, axis=1)

    one_direction(qf_ref, ff_ref, vf_ref, mf_ref, lvf_ref, of_ref, sf_ref, False)
    one_direction(qb_ref, fb_ref, vb_ref, mb_ref, lvb_ref, ob_ref, sb_ref, True)


def _hgrn2(p_qig, p_f, lb, *, seq, heads_per_step=2):
    n_batch, t, _ = p_qig.shape
    d_k = lb.shape[-1]
    n_heads = d_k // HG_DK
    hb = heads_per_step if n_heads % heads_per_step == 0 else 1
    wb = hb * HG_DK
    tt = t - seq
    assert seq % tt == 0 and tt % HG_CHUNK == 0
    assert p_qig.shape[2] == 3 * d_k and p_f.shape[2] == 2 * d_k
    assert (tt // HG_CHUNK) & (tt // HG_CHUNK - 1) == 0
    nt = t // tt
    ncol = d_k // wb
    mf, mb, lvf, lvb = _hgrn2_tables(tt)

    def tile_f(s):
        return jnp.where(s == 0, nt - 1, s - 1)

    def tile_b(s):
        return jnp.where(s == 0, nt - 1, nt - 1 - s)

    def col(tile, grp):
        return pl.BlockSpec((None, tt, wb), lambda b, g, s: (b, tile(s), grp * ncol + g))

    const = lambda shape: pl.BlockSpec(shape, lambda b, g, s: (0, 0))
    out_sd = jax.ShapeDtypeStruct((n_batch, t, d_k), F32)
    return pl.pallas_call(
        functools.partial(_hgrn2_kernel, heads=hb),
        out_shape=(out_sd, out_sd),
        grid=(n_batch, ncol, nt),
        in_specs=[col(tile_f, 0), col(tile_f, 0), col(tile_f, 1),
                  col(tile_b, 0), col(tile_b, 1), col(tile_b, 1),
                  pl.BlockSpec((1, wb), lambda b, g, s: (0, g)),
                  const(mf.shape), const(mb.shape), const(lvf.shape), const(lvb.shape)],
        out_specs=(pl.BlockSpec((None, tt, wb), lambda b, g, s: (b, tile_f(s), g)),
                   pl.BlockSpec((None, tt, wb), lambda b, g, s: (b, tile_b(s), g))),
        scratch_shapes=[pltpu.VMEM((hb, HG_DK, HG_DK), F32), pltpu.VMEM((hb, HG_DK, HG_DK), F32)],
        compiler_params=_cparams("arbitrary", "arbitrary", "arbitrary"),
        name="hgrn2_scan",
    )(p_qig, p_f, p_qig, p_qig, p_f, p_qig, lb.reshape(1, d_k), jnp.asarray(mf, BF16),
      jnp.asarray(mb, BF16), jnp.asarray(lvf), jnp.asarray(lvb))


def _outproj_hg_kernel(of_ref, ob_ref, gate_ref, nw_ref, w_ref, x_ref, g_ref, o_ref, y_ref,
                       *, n_batch, seq, t_total):
    tm = x_ref.shape[0]
    nw = nw_ref[...]
    dv = nw.shape[1]
    for h in range(of_ref.shape[1] // dv):
        cs = slice(h * dv, (h + 1) * dv)
        o = of_ref[:, cs] + ob_ref[:, cs]
        y = o * lax.rsqrt(jnp.mean(o * o, axis=-1, keepdims=True) + EPS) * nw
        y_ref[:, cs] = (y * _silu(gate_ref[:, cs].astype(F32))).astype(BF16)
    y = _dot(y_ref[...], w_ref[...])

    def finish(r0, r1, rid):
        o_ref[r0:r1, :] = x_ref[r0:r1, :] + g_ref[pl.ds(rid, 1), :] * y[r0:r1, :]

    _for_row_groups(tm, t_total, seq, n_batch, finish)


def _out_proj_hg(o_f, o_b, p, norm_w, w_out, xs, mods, layer, w_layer, *, seq):
    n_batch, t, d = xs.shape
    d_v = o_f.shape[2]
    tm = _row_tile(t) // 2
    assert tm % 16 == 0
    gate_blk = (p.shape[2] - d_v) // d_v
    kern = functools.partial(_outproj_hg_kernel, n_batch=n_batch, seq=seq, t_total=t)
    rows = lambda width, cblk: pl.BlockSpec((None, tm, width), lambda b, i: (b, i, cblk))
    return pl.pallas_call(
        kern,
        out_shape=jax.ShapeDtypeStruct((n_batch, t, d), F32),
        grid=(n_batch, t // tm),
        in_specs=[
            rows(d_v, 0), rows(d_v, 0), rows(d_v, gate_blk),
            pl.BlockSpec((1, norm_w.shape[0]), lambda b, i: (0, 0)),
            pl.BlockSpec((None, d_v, d), lambda b, i: (w_layer, 0, 0)),
            rows(d, 0),
            _mod_spec(d, 2, layer, 2),
        ],
        out_specs=rows(d, 0),
        scratch_shapes=[pltpu.VMEM((tm, d_v), BF16)],
        compiler_params=_cparams("arbitrary", "arbitrary"),
        name="out_proj_hgrn2",
    )(o_f, o_b, p, norm_w.reshape(1, -1), w_out, xs, mods)


def _final_norm_kernel(x_ref, w_ref, o_ref):
    x = x_ref[...]
    o_ref[...] = x * lax.rsqrt(jnp.mean(x * x, axis=-1, keepdims=True) + EPS) * w_ref[...]


def _final_norm(xs, w, *, seq):
    n_batch, t, d = xs.shape
    tm = t - seq
    return pl.pallas_call(
        _final_norm_kernel,
        out_shape=jax.ShapeDtypeStruct((n_batch, seq, d), F32),
        grid=(n_batch, seq // tm),
        in_specs=[pl.BlockSpec((None, tm, d), lambda b, i: (b, i, 0)),
                  pl.BlockSpec((1, d), lambda b, i: (0, 0))],
        out_specs=pl.BlockSpec((None, tm, d), lambda b, i: (b, i, 0)),
        compiler_params=_cparams("arbitrary", "arbitrary"),
        name="final_norm",
    )(xs, w.reshape(1, d))


def _rot_columns(w):
    q = QK_ROPE // 4
    return jnp.concatenate([-w[..., q:2 * q], w[..., :q], -w[..., 3 * q:], w[..., 2 * q:3 * q]], axis=-1)


def _rope_tables(seq, n_ctx):
    t = np.arange(seq)
    row, colm = t // GRID_W, t % GRID_W
    half = QK_ROPE // 2
    inv_freq = ROPE_BASE ** (-np.arange(0, half, 2, dtype=np.float32) / half)
    ang_r = row.astype(np.float32)[:, None] * inv_freq
    ang_c = colm.astype(np.float32)[:, None] * inv_freq
    ang = jnp.asarray(np.concatenate([ang_r, ang_r, ang_c, ang_c], axis=-1), F32)
    cos = jnp.concatenate([jnp.cos(ang), jnp.ones((n_ctx, QK_ROPE), F32)], axis=0)
    sin = jnp.concatenate([jnp.sin(ang), jnp.zeros((n_ctx, QK_ROPE), F32)], axis=0)
    return cos, sin


def kernel(x, c, ctx, c_ctx, ada_w, ada_b, norm_mix_w, norm_mlp_w, ab_w_in, ab_w_out, lru_conv_w, lru_conv_b, lru_w_a, lru_b_a, lru_w_x, lru_b_x, lru_lambda, mla_q_norm_w, mla_w_uq, mla_kv_norm_w, mla_w_ukv, hg_w_in, hg_lb_logits, hg_norm_w, hg_w_out, mlp_w1, mlp_w2, final_norm_w):
    n_batch, seq, d = x.shape
    n_ctx = ctx.shape[1]
    depth = ada_w.shape[0]
    assert n_batch + 1 <= SUBLANES

    xs = jnp.concatenate([x, ctx], axis=1)
    cond8 = jnp.zeros((SUBLANES, d), F32).at[:n_batch].set(c).at[n_batch].set(c_ctx)
    mods = _ada_all(cond8, ada_w, ada_b)
    lb_all = _lower_bounds(hg_lb_logits)
    cos_t, sin_t = _rope_tables(seq, n_ctx)

    d_lru = lru_conv_w.shape[-1]
    q_lora = mla_q_norm_w.shape[-1]
    col_kr = ab_w_in.shape[2] - QK_ROPE
    ab_w_in_ext = jnp.concatenate([ab_w_in, _rot_columns(ab_w_in[..., col_kr:])], axis=-1).astype(BF16)
    n_ab = ab_w_in_ext.shape[2]
    tn_ab = n_ab // 5 if n_ab % (5 * LANES) == 0 else n_ab
    d_hg = hg_lb_logits.shape[-1]
    hg_w_in_b = jnp.concatenate([hg_w_in[..., :d_hg], hg_w_in[..., 3 * d_hg:], hg_w_in[..., d_hg:3 * d_hg]],
                                axis=-1).astype(BF16)
    tn_hg = 1024 if d_hg % 1024 == 0 else d_hg
    ab_w_out_b = ab_w_out.astype(BF16)
    hg_w_out_b = hg_w_out.astype(BF16)
    w1_b = mlp_w1.astype(BF16)
    w2_b = mlp_w2.astype(BF16)
    wq = mla_w_uq.reshape(-1, q_lora, MLA_HEADS, QK_NOPE + QK_ROPE)
    wq_ext = jnp.concatenate([wq, _rot_columns(wq[..., QK_NOPE:])], axis=-1)
    wq_ext = wq_ext.reshape(wq.shape[0], q_lora, -1).astype(BF16)
    wkv_b = mla_w_ukv.astype(BF16)
    lru_w_a_b = lru_w_a.astype(BF16)
    lru_w_x_b = lru_w_x.astype(BF16)

    for l in range(depth):
        if l % 2 == 0:
            e = l // 2
            p = _in_proj(xs, norm_mix_w[l], mods, ab_w_in_ext, l, e, seq=seq, tn=tn_ab)
            ya = _lru(p, lru_conv_w[e], lru_conv_b[e], lru_w_a_b[e], lru_b_a[e],
                      lru_w_x_b[e], lru_b_x[e], lru_lambda[e], seq=seq)
            q, k, vt = _mla_qkv(p, mla_q_norm_w[e], mla_kv_norm_w[e], wq_ext[e], wkv_b[e],
                                cos_t, sin_t, seq=seq, col0=2 * d_lru)
            yb = _attention(q, k, vt, seq=seq)
            xs = _out_proj_ab(ya, yb, ab_w_out_b, xs, mods, l, e, seq=seq)
        else:
            o = l // 2
            p_qig, p_f = _in_proj(xs, norm_mix_w[l], mods, hg_w_in_b, l, o, seq=seq, tn=tn_hg,
                                  n_f32_cols=2 * d_hg)
            o_f, o_b = _hgrn2(p_qig, p_f, lb_all[l], seq=seq)
            xs = _out_proj_hg(o_f, o_b, p_qig, hg_norm_w[o], hg_w_out_b, xs, mods, l, o, seq=seq)
        xs = _mlp(xs, norm_mlp_w[l], mods, w1_b, w2_b, l, seq=seq)
    return _final_norm(xs, final_norm_w, seq=seq)
```

```python
import functools

import numpy as np
import jax
import jax.numpy as jnp
from jax import lax
from jax.experimental import pallas as pl
from jax.experimental.pallas import tpu as pltpu

GRID_W = 64
EPS = 1e-6
N_MOD = 6
LRU_HEADS = 8
CONV_W = 4
CONV_LEFT = CONV_W // 2
LRU_C = 8.0
MLA_HEADS = 8
QK_NOPE = 128
QK_ROPE = 64
V_HEAD = 128
ROPE_BASE = 10000.0
HG_DK = 128
HG_CHUNK = 64

LANES = 128
SUBLANES = 8
VMEM_LIMIT_BYTES = 56 * 1024 * 1024
NORM_ROW_CHUNK = 32
ATTN_KEY_CHUNK = 1024
ATTN_Q_TILE = 512
ATTN_TILES_PER_STEP = 2
MLP_LATENT_ROW_TILE = 512

F32 = jnp.float32
BF16 = jnp.bfloat16

HG_LEVELS = int(np.log2(HG_CHUNK))


def _cparams(*sem):
    return pltpu.CompilerParams(dimension_semantics=sem, vmem_limit_bytes=VMEM_LIMIT_BYTES)


def _dot(a, b):
    return jnp.dot(a, b, preferred_element_type=F32)


def _dot_nt(a, b):
    return lax.dot_general(a, b, (((1,), (1,)), ((), ())), preferred_element_type=F32)


def _dot_tn(a, b):
    return lax.dot_general(a, b, (((0,), (0,)), ((), ())), preferred_element_type=F32)


def _sigmoid(x):
    return 1.0 / (1.0 + jnp.exp(-x))


def _silu(x):
    return x * _sigmoid(x)


def _gelu_tanh(x):
    c = np.float32(np.sqrt(2.0 / np.pi))
    return 0.5 * x * (1.0 + jnp.tanh(c * (x + np.float32(0.044715) * (x * x * x))))


def _softplus(x):
    return jnp.maximum(x, 0.0) + jnp.log1p(jnp.exp(-jnp.abs(x)))


def _row_tile(t_total):
    tm = t_total // 8
    assert tm * 8 == t_total and tm % 16 == 0, t_total
    return tm


def _ada_kernel(c_ref, w_ref, b_ref, o_ref):
    a = _silu(c_ref[...]).astype(BF16)
    o_ref[...] = _dot(a, w_ref[...].astype(BF16)) + b_ref[...]


def _ada_all(cond8, ada_w, ada_b):
    n_layers, d, n6 = ada_w.shape
    tn = 1024 if n6 % 1024 == 0 else n6 // N_MOD
    return pl.pallas_call(
        _ada_kernel,
        out_shape=jax.ShapeDtypeStruct((n_layers, SUBLANES, n6), F32),
        grid=(n_layers, n6 // tn),
        in_specs=[
            pl.BlockSpec((SUBLANES, d), lambda l, j: (0, 0)),
            pl.BlockSpec((None, d, tn), lambda l, j: (l, 0, j)),
            pl.BlockSpec((None, 1, tn), lambda l, j: (l, 0, j)),
        ],
        out_specs=pl.BlockSpec((None, SUBLANES, tn), lambda l, j: (l, 0, j)),
        compiler_params=_cparams("arbitrary", "arbitrary"),
        name="ada_params",
    )(cond8, ada_w, ada_b.reshape(n_layers, 1, n6))


def _lb_kernel(x_ref, o_ref):
    n = x_ref.shape[0]
    rows = [x_ref[pl.ds(l, 1), :] for l in range(n)]
    mx = functools.reduce(jnp.maximum, rows)
    es = [jnp.exp(r - mx) for r in rows]
    tot = functools.reduce(lambda a, b: a + b, es)
    ps = [e / tot for e in es]
    acc = ps[0]
    o_ref[pl.ds(0, 1), :] = acc - ps[0]
    for l in range(1, n):
        acc = acc + ps[l]
        o_ref[pl.ds(l, 1), :] = acc - ps[0]


def _lower_bounds(logits):
    return pl.pallas_call(
        _lb_kernel, out_shape=jax.ShapeDtypeStruct(logits.shape, F32), name="hgrn2_lower_bounds"
    )(logits.astype(F32))


def _for_row_groups(tm, t_total, seq, n_batch, fn):
    b = pl.program_id(0)
    i = pl.program_id(1)
    i_mixed, split = divmod(seq, tm)
    if i_mixed > 0:
        pl.when(i < i_mixed)(lambda: fn(0, tm, b))
    if i_mixed < t_total // tm:
        @pl.when(i == i_mixed)
        def _():
            if split > 0:
                fn(0, split, b)
            fn(split, tm, n_batch)
    if i_mixed + 1 < t_total // tm:
        pl.when(i > i_mixed)(lambda: fn(0, tm, n_batch))


def _norm_modulate_rows(x_ref, nw_ref, sh_ref, sc_ref, h_ref, r0, r1, rid):
    gain = nw_ref[...] * (1.0 + sc_ref[pl.ds(rid, 1), :])
    shift = sh_ref[pl.ds(rid, 1), :]
    rc = NORM_ROW_CHUNK
    for r in range(r0, r1, rc):
        x = x_ref[r:min(r + rc, r1), :]
        ms = jnp.mean(x * x, axis=-1, keepdims=True)
        h_ref[r:min(r + rc, r1), :] = (x * lax.rsqrt(ms + EPS) * gain + shift).astype(h_ref.dtype)


def _mod_spec(d, k, layer, nargs):
    if nargs == 2:
        return pl.BlockSpec((None, SUBLANES, d), lambda b, i: (layer, 0, k))
    return pl.BlockSpec((None, SUBLANES, d), lambda b, i, j: (layer, 0, k))


def _inproj_kernel(*refs, n_batch, seq, t_total, f32_tiles, has_extra):
    x_ref, nw_ref, sh_ref, sc_ref, w_ref = refs[:5]
    h_ref = refs[-1]
    rest = list(refs[5:-1])
    wx_ref = rest.pop(0) if has_extra else None
    lo_ref = rest.pop(0)
    hi_ref = rest.pop(0) if f32_tiles else None
    ox_ref = rest.pop(0) if has_extra else None
    tm = x_ref.shape[0]
    j = pl.program_id(2)

    @pl.when(j == 0)
    def _():
        _for_row_groups(tm, t_total, seq, n_batch, functools.partial(
            _norm_modulate_rows, x_ref, nw_ref, sh_ref, sc_ref, h_ref))
        if has_extra:
            ox_ref[...] = _dot(h_ref[...], wx_ref[...]).astype(ox_ref.dtype)

    y = _dot(h_ref[...], w_ref[...])
    if not f32_tiles:
        lo_ref[...] = y.astype(lo_ref.dtype)
    else:
        is_f32 = jnp.logical_and(j >= f32_tiles[0], j < f32_tiles[1])

        @pl.when(jnp.logical_not(is_f32))
        def _():
            lo_ref[...] = y.astype(lo_ref.dtype)

        @pl.when(is_f32)
        def _():
            hi_ref[...] = y.astype(hi_ref.dtype)


def _in_proj(xs, norm_w, mods, w, layer, w_layer, *, seq, tn, n_cols=None, f32_cols=None, w_extra=None):
    n_batch, t, d = xs.shape
    n_out = w.shape[2] if n_cols is None else n_cols
    tm = _row_tile(t) * 2
    assert n_out % tn == 0
    n_tiles = n_out // tn
    j0, j1 = (0, 0) if f32_cols is None else (f32_cols[0] // tn, f32_cols[1] // tn)
    assert f32_cols is None or (f32_cols[0] % tn == 0 and f32_cols[1] % tn == 0 and j0 < j1 <= n_tiles)
    n_hi = j1 - j0
    kern = functools.partial(_inproj_kernel, n_batch=n_batch, seq=seq, t_total=t,
                             f32_tiles=(j0, j1) if n_hi else None, has_extra=w_extra is not None)

    def lo_idx(j):
        return jnp.where(j < j0, j, jnp.where(j < j1, max(j0 - 1, 0), j - n_hi))

    in_specs = [
        pl.BlockSpec((None, tm, d), lambda b, i, j: (b, i, 0)),
        pl.BlockSpec((1, d), lambda b, i, j: (0, 0)),
        _mod_spec(d, 0, layer, 3),
        _mod_spec(d, 1, layer, 3),
        pl.BlockSpec((None, d, tn), lambda b, i, j: (w_layer, 0, j)),
    ]
    args = [xs, norm_w.reshape(1, d), mods, mods, w]
    out_shape = [jax.ShapeDtypeStruct((n_batch, t, (n_tiles - n_hi) * tn), BF16)]
    out_specs = [pl.BlockSpec((None, tm, tn), lambda b, i, j: (b, i, lo_idx(j)))]
    if n_hi:
        out_shape.append(jax.ShapeDtypeStruct((n_batch, t, n_hi * tn), F32))
        out_specs.append(pl.BlockSpec((None, tm, tn), lambda b, i, j: (b, i, jnp.clip(j - j0, 0, n_hi - 1))))
    if w_extra is not None:
        in_specs.append(pl.BlockSpec((None, d, LANES), lambda b, i, j: (w_layer, 0, 0)))
        args.append(w_extra)
        out_shape.append(jax.ShapeDtypeStruct((n_batch, t, LANES), BF16))
        out_specs.append(pl.BlockSpec((None, tm, LANES), lambda b, i, j: (b, i, 0)))
    outs = pl.pallas_call(
        kern,
        out_shape=out_shape,
        grid=(n_batch, t // tm, n_tiles),
        in_specs=in_specs,
        out_specs=out_specs,
        scratch_shapes=[pltpu.VMEM((tm, d), BF16)],
        compiler_params=_cparams("arbitrary", "arbitrary", "arbitrary"),
        name="in_proj",
    )(*args)
    return outs if len(outs) > 1 else outs[0]


def _mlp_kernel(x_ref, nw_ref, sh_ref, sc_ref, g_ref, w1_ref, w2_ref, *rest, n_batch, seq, t_total):
    fw_ref = rest[0] if len(rest) == 4 else None
    o_ref, h_ref, acc_ref = rest[-3:]
    f = pl.program_id(2)
    tm = x_ref.shape[0]

    @pl.when(f == 0)
    def _():
        _for_row_groups(tm, t_total, seq, n_batch, functools.partial(
            _norm_modulate_rows, x_ref, nw_ref, sh_ref, sc_ref, h_ref))
        acc_ref[...] = jnp.zeros_like(acc_ref)

    a = jnp.maximum(_dot(h_ref[...], w1_ref[...]), 0.0)
    acc_ref[...] += _dot((a * a).astype(BF16), w2_ref[...])

    @pl.when(f == pl.num_programs(2) - 1)
    def _():
        def finish(r0, r1, rid):
            rc = NORM_ROW_CHUNK if fw_ref is not None else r1 - r0
            for r in range(r0, r1, rc):
                rs = slice(r, min(r + rc, r1))
                y = x_ref[rs, :] + g_ref[pl.ds(rid, 1), :] * acc_ref[rs, :]
                if fw_ref is not None:
                    y = y * lax.rsqrt(jnp.mean(y * y, axis=-1, keepdims=True) + EPS) * fw_ref[...]
                o_ref[rs, :] = y

        _for_row_groups(tm, t_total, seq, n_batch, finish)


def _mlp(xs, norm_w, mods, w1, w2, layer, *, seq, final_norm_w=None):
    n_batch, t, d = xs.shape
    d_ff = w1.shape[2]
    if final_norm_w is None:
        rows, tm = t, _row_tile(t)
    else:
        rows = seq
        tm = MLP_LATENT_ROW_TILE if seq % MLP_LATENT_ROW_TILE == 0 else t - seq
        assert seq % tm == 0
    tf = 1024 if d_ff % 1024 == 0 else d_ff
    kern = functools.partial(_mlp_kernel, n_batch=n_batch, seq=seq, t_total=rows)
    in_specs = [
        pl.BlockSpec((None, tm, d), lambda b, i, f: (b, i, 0)),
        pl.BlockSpec((1, d), lambda b, i, f: (0, 0)),
        _mod_spec(d, 3, layer, 3),
        _mod_spec(d, 4, layer, 3),
        _mod_spec(d, 5, layer, 3),
        pl.BlockSpec((None, d, tf), lambda b, i, f: (layer, 0, f)),
        pl.BlockSpec((None, tf, d), lambda b, i, f: (layer, f, 0)),
    ]
    args = [xs, norm_w.reshape(1, d), mods, mods, mods, w1, w2]
    if final_norm_w is not None:
        in_specs.append(pl.BlockSpec((1, d), lambda b, i, f: (0, 0)))
        args.append(final_norm_w.reshape(1, d))
    return pl.pallas_call(
        kern,
        out_shape=jax.ShapeDtypeStruct((n_batch, rows, d), F32),
        grid=(n_batch, rows // tm, d_ff // tf),
        in_specs=in_specs,
        out_specs=pl.BlockSpec((None, tm, d), lambda b, i, f: (b, i, 0)),
        scratch_shapes=[pltpu.VMEM((tm, d), BF16), pltpu.VMEM((tm, d), F32)],
        compiler_params=_cparams("arbitrary", "arbitrary", "arbitrary"),
        name="mlp",
    )(*args)


def _outproj_ab_kernel(ya_ref, yb_ref, w_ref, x_ref, g_ref, o_ref, *, n_batch, seq, t_total):
    tm = x_ref.shape[0]
    da = ya_ref.shape[1]
    y = _dot(ya_ref[...], w_ref[0:da, :]) + _dot(yb_ref[...], w_ref[da:, :])

    def finish(r0, r1, rid):
        o_ref[r0:r1, :] = x_ref[r0:r1, :] + g_ref[pl.ds(rid, 1), :] * y[r0:r1, :]

    _for_row_groups(tm, t_total, seq, n_batch, finish)


def _out_proj_ab(ya, yb, w_out, xs, mods, layer, w_layer, *, seq):
    n_batch, t, d = xs.shape
    da, db = ya.shape[2], yb.shape[2]
    tm = _row_tile(t)
    kern = functools.partial(_outproj_ab_kernel, n_batch=n_batch, seq=seq, t_total=t)
    return pl.pallas_call(
        kern,
        out_shape=jax.ShapeDtypeStruct((n_batch, t, d), F32),
        grid=(n_batch, t // tm),
        in_specs=[
            pl.BlockSpec((None, tm, da), lambda b, i: (b, i, 0)),
            pl.BlockSpec((None, tm, db), lambda b, i: (b, i, 0)),
            pl.BlockSpec((None, da + db, d), lambda b, i: (w_layer, 0, 0)),
            pl.BlockSpec((None, tm, d), lambda b, i: (b, i, 0)),
            _mod_spec(d, 2, layer, 2),
        ],
        out_specs=pl.BlockSpec((None, tm, d), lambda b, i: (b, i, 0)),
        compiler_params=_cparams("arbitrary", "arbitrary"),
        name="out_proj_ab",
    )(ya, yb, w_out, xs, mods)


def _lru_kernel(gate_ref, u_ref, cw_ref, cb_ref, wa_ref, ba_ref, wx_ref, bx_ref, lam_ref, o_ref,
                upad_ref, af_ref, bf_ref, ab_ref, bb_ref, hf_ref, hb_ref, *, seq, chunk):
    t_total = u_ref.shape[0]
    n_ctx = t_total - seq
    pad = SUBLANES
    cw = cw_ref[...]
    cb = cb_ref[...]
    sp = _softplus(-lam_ref[...])
    zeros_pad = jnp.zeros((pad, LANES), F32)

    def gates(uc, d):
        ucb = uc.astype(BF16)
        r = _sigmoid(_dot(ucb, wa_ref[d]) + ba_ref[pl.ds(d, 1), :])
        i = _sigmoid(_dot(ucb, wx_ref[d]) + bx_ref[pl.ds(d, 1), :])
        log_a = (-LRU_C) * r * sp[d:d + 1, :]
        a = jnp.exp(log_a)
        return a, jnp.sqrt(1.0 - a * a) * (i * uc)

    def prepare(src0, dst0, n):
        upad_ref[pl.ds(0, pad), :] = zeros_pad
        upad_ref[pl.ds(pad + n, pad), :] = zeros_pad
        upad_ref[pl.ds(pad, n), :] = u_ref[pl.ds(src0, n), :].astype(F32)
        ch = min(chunk, n)

        def body(c, carry):
            r0 = pl.multiple_of(c * ch, SUBLANES)
            win = upad_ref[pl.ds(r0, ch + 2 * pad), :]
            uc = cb
            for j in range(CONV_W):
                off = pad + j - CONV_LEFT
                uc = uc + win[off:off + ch, :] * cw[j:j + 1, :]
            a0, b0 = gates(uc, 0)
            a1, b1 = gates(uc, 1)
            d0 = pl.multiple_of(dst0 + r0, SUBLANES)
            af_ref[pl.ds(d0, ch), :] = a0
            bf_ref[pl.ds(d0, ch), :] = b0
            ab_ref[pl.ds(d0, ch), :] = a1
            bb_ref[pl.ds(d0, ch), :] = b1
            return carry

        lax.fori_loop(0, n // ch, body, 0)

    prepare(seq, 0, n_ctx)
    prepare(0, n_ctx, seq)

    row = lax.broadcasted_iota(jnp.int32, (SUBLANES, LANES), 0)

    def scan_fwd(a, b):
        for s in (1, 2, 4):
            m = row >= s
            a_s = pltpu.roll(a, s, 0)
            b_s = pltpu.roll(b, s, 0)
            b = jnp.where(m, a * b_s + b, b)
            a = jnp.where(m, a * a_s, a)
        return a, b

    def scan_bwd(a, b):
        for s in (1, 2, 4):
            m = row < SUBLANES - s
            a_s = pltpu.roll(a, SUBLANES - s, 0)
            b_s = pltpu.roll(b, SUBLANES - s, 0)
            b = jnp.where(m, a * b_s + b, b)
            a = jnp.where(m, a * a_s, a)
        return a, b

    def scan_range(g0, n_groups, carry):
        def body(g, c):
            cf, cbk = c
            rf = pl.multiple_of((g0 + g) * SUBLANES, SUBLANES)
            a, b = scan_fwd(af_ref[pl.ds(rf, SUBLANES), :], bf_ref[pl.ds(rf, SUBLANES), :])
            hf = a * cf + b
            hf_ref[pl.ds(rf, SUBLANES), :] = hf
            rb = pl.multiple_of((g0 + n_groups - 1 - g) * SUBLANES, SUBLANES)
            a, b = scan_bwd(ab_ref[pl.ds(rb, SUBLANES), :], bb_ref[pl.ds(rb, SUBLANES), :])
            hb = a * cbk + b
            hb_ref[pl.ds(rb, SUBLANES), :] = hb
            cf = jnp.broadcast_to(hf[SUBLANES - 1:SUBLANES, :], (SUBLANES, LANES))
            cbk = jnp.broadcast_to(hb[0:1, :], (SUBLANES, LANES))
            return cf, cbk

        return lax.fori_loop(0, n_groups, body, carry, unroll=4)

    zero = jnp.zeros((SUBLANES, LANES), F32)
    carry = scan_range(0, n_ctx // SUBLANES, (zero, zero))
    scan_range(n_ctx // SUBLANES, seq // SUBLANES, carry)

    def finish(src0, dst0, n):
        ch = min(chunk, n)

        def body(c, carry):
            r0 = c * ch
            s0 = pl.multiple_of(src0 + r0, SUBLANES)
            d0 = pl.multiple_of(dst0 + r0, SUBLANES)
            h = hf_ref[pl.ds(s0, ch), :] + hb_ref[pl.ds(s0, ch), :]
            gate = gate_ref[pl.ds(d0, ch), :].astype(F32)
            o_ref[pl.ds(d0, ch), :] = (h * _gelu_tanh(gate)).astype(o_ref.dtype)
            return carry

        lax.fori_loop(0, n // ch, body, 0)

    finish(0, seq, n_ctx)
    finish(n_ctx, 0, seq)


def _lru(p, conv_w, conv_b, w_a, b_a, w_x, b_x, lam, *, seq):
    n_batch, t, _ = p.shape
    d_lru = conv_w.shape[1]
    n_heads = w_a.shape[1]
    assert d_lru // n_heads == LANES
    n_ctx = t - seq
    chunk = 256
    assert seq % min(chunk, seq) == 0 and n_ctx % min(chunk, n_ctx) == 0
    kern = functools.partial(_lru_kernel, seq=seq, chunk=chunk)
    vec = lambda rows: pl.BlockSpec((rows, LANES), lambda b, h: (0, h))
    wspec = pl.BlockSpec((2, None, LANES, LANES), lambda b, h: (0, h, 0, 0))
    scratch_rows = pltpu.VMEM((t, LANES), F32)
    return pl.pallas_call(
        kern,
        out_shape=jax.ShapeDtypeStruct((n_batch, t, d_lru), BF16),
        grid=(n_batch, n_heads),
        in_specs=[
            pl.BlockSpec((None, t, LANES), lambda b, h: (b, 0, h)),
            pl.BlockSpec((None, t, LANES), lambda b, h: (b, 0, n_heads + h)),
            vec(CONV_W), vec(1), wspec, vec(2), wspec, vec(2), vec(2),
        ],
        out_specs=pl.BlockSpec((None, t, LANES), lambda b, h: (b, 0, h)),
        scratch_shapes=[pltpu.VMEM((max(seq, n_ctx) + 2 * SUBLANES, LANES), F32),
                        scratch_rows, scratch_rows, scratch_rows, scratch_rows,
                        scratch_rows, scratch_rows],
        compiler_params=_cparams("arbitrary", "arbitrary"),
        name="rglru",
    )(p, p, conv_w, conv_b.reshape(1, d_lru), w_a, b_a, w_x, b_x, lam)


def _mla_qkv_kernel(cq_ref, ckv_ref, kr_ref, qn_ref, kvn_ref, wq_ref, wkv_ref, cos_ref, sin_ref,
                    q_ref, k_ref, v_ref, *, scale):
    def rms(x, w):
        return (x * lax.rsqrt(jnp.mean(x * x, axis=-1, keepdims=True) + EPS) * w).astype(BF16)

    cos = cos_ref[...]
    sin = sin_ref[...]
    q_all = _dot(rms(cq_ref[...].astype(F32), qn_ref[...]), wq_ref[...])
    kv_all = _dot(rms(ckv_ref[...].astype(F32), kvn_ref[...]), wkv_ref[...])
    kr = kr_ref[...].astype(F32)
    kr_rot = kr[:, :QK_ROPE] * cos + kr[:, QK_ROPE:] * sin
    wq_head = QK_NOPE + 2 * QK_ROPE
    wkv_head = QK_NOPE + V_HEAD
    for h in range(q_ref.shape[0]):
        qh = q_all[:, h * wq_head:(h + 1) * wq_head]
        qr = qh[:, QK_NOPE:QK_NOPE + QK_ROPE] * cos + qh[:, QK_NOPE + QK_ROPE:] * sin
        q_ref[h, :, 0:QK_NOPE] = (qh[:, :QK_NOPE] * scale).astype(BF16)
        q_ref[h, :, QK_NOPE:] = (qr * scale).astype(BF16)
        kvh = kv_all[:, h * wkv_head:(h + 1) * wkv_head]
        k_ref[h, :, 0:QK_NOPE] = kvh[:, :QK_NOPE].astype(BF16)
        k_ref[h, :, QK_NOPE:] = kr_rot.astype(BF16)
        v_ref[h] = kvh[:, QK_NOPE:].T.astype(BF16)


def _mla_qkv(p, p_kr, q_norm_w, kv_norm_w, wq_ext, wkv, cos_t, sin_t, *, seq, col0):
    n_batch, t, _ = p.shape
    q_lora = q_norm_w.shape[0]
    kv_lora = kv_norm_w.shape[0]
    assert q_lora == kv_lora and col0 % q_lora == 0 and p_kr.shape[2] == 2 * QK_ROPE == LANES
    n_ctx = t - seq
    tm = n_ctx
    nh = MLA_HEADS
    dqk = QK_NOPE + QK_ROPE
    scale = np.float32(dqk ** -0.5 * np.log2(np.e))
    cq_blk = col0 // q_lora
    full = lambda shape: pl.BlockSpec(shape, lambda b, i: (0,) * len(shape))
    return pl.pallas_call(
        functools.partial(_mla_qkv_kernel, scale=scale),
        out_shape=(jax.ShapeDtypeStruct((n_batch, nh, t, dqk), BF16),
                   jax.ShapeDtypeStruct((n_batch, nh, t, dqk), BF16),
                   jax.ShapeDtypeStruct((n_batch, nh, V_HEAD, t), BF16)),
        grid=(n_batch, t // tm),
        in_specs=[
            pl.BlockSpec((None, tm, q_lora), lambda b, i: (b, i, cq_blk)),
            pl.BlockSpec((None, tm, kv_lora), lambda b, i: (b, i, cq_blk + 1)),
            pl.BlockSpec((None, tm, LANES), lambda b, i: (b, i, 0)),
            full((1, q_lora)), full((1, kv_lora)), full(wq_ext.shape), full(wkv.shape),
            pl.BlockSpec((tm, QK_ROPE), lambda b, i: (i, 0)),
            pl.BlockSpec((tm, QK_ROPE), lambda b, i: (i, 0)),
        ],
        out_specs=(pl.BlockSpec((None, nh, tm, dqk), lambda b, i: (b, 0, i, 0)),
                   pl.BlockSpec((None, nh, tm, dqk), lambda b, i: (b, 0, i, 0)),
                   pl.BlockSpec((None, nh, V_HEAD, tm), lambda b, i: (b, 0, 0, i))),
        compiler_params=_cparams("arbitrary", "arbitrary"),
        name="mla_qkv",
    )(p, p, p_kr, q_norm_w.reshape(1, q_lora), kv_norm_w.reshape(1, kv_lora), wq_ext, wkv, cos_t, sin_t)


def _softmax_values(qs, k_ref, vt_ref, s_ref):
    t = k_ref.shape[0]
    bounds = list(range(0, t, ATTN_KEY_CHUNK)) + [t]
    chunks = list(zip(bounds[:-1], bounds[1:]))
    n = len(qs)
    ms, ls, outs = [None] * n, [None] * n, [None] * n
    for stage in range(n + 1):
        for a, b in chunks:
            if stage < n:
                s = _dot_nt(k_ref[a:b, :], qs[stage])
                s_ref[stage, a:b, :] = s
                cm = jnp.max(s, axis=0, keepdims=True)
                ms[stage] = cm if ms[stage] is None else jnp.maximum(ms[stage], cm)
            if stage >= 1:
                i = stage - 1
                p = jnp.exp2(s_ref[i, a:b, :] - ms[i])
                ps = jnp.sum(p, axis=0, keepdims=True)
                pv = _dot(vt_ref[:, a:b], p.astype(BF16))
                ls[i] = ps if ls[i] is None else ls[i] + ps
                outs[i] = pv if outs[i] is None else outs[i] + pv
    return [(outs[i] * (1.0 / ls[i])).T for i in range(n)]


def _attn_kernel(q_ref, k_ref, vt_ref, o_ref, s_ref):
    n = s_ref.shape[0]
    tq = q_ref.shape[0] // n
    outs = _softmax_values([q_ref[i * tq:(i + 1) * tq, :] for i in range(n)], k_ref, vt_ref, s_ref)
    for i in range(n):
        o_ref[i * tq:(i + 1) * tq, :] = outs[i].astype(o_ref.dtype)


def _attention(q, k, vt, *, seq):
    n_batch, nh, t, dqk = q.shape
    n_ctx = t - seq
    tq = ATTN_Q_TILE if seq % ATTN_Q_TILE == 0 else n_ctx
    assert seq % tq == 0 and seq % n_ctx == 0
    nq = ATTN_TILES_PER_STEP if seq % (ATTN_TILES_PER_STEP * tq) == 0 else 1
    y_lat = pl.pallas_call(
        _attn_kernel,
        out_shape=jax.ShapeDtypeStruct((n_batch, seq, nh * V_HEAD), BF16),
        grid=(n_batch, nh, seq // (nq * tq)),
        in_specs=[
            pl.BlockSpec((None, None, nq * tq, dqk), lambda b, h, i: (b, h, i, 0)),
            pl.BlockSpec((None, None, t, dqk), lambda b, h, i: (b, h, 0, 0)),
            pl.BlockSpec((None, None, V_HEAD, t), lambda b, h, i: (b, h, 0, 0)),
        ],
        out_specs=pl.BlockSpec((None, nq * tq, V_HEAD), lambda b, h, i: (b, i, h)),
        scratch_shapes=[pltpu.VMEM((nq, t, tq), F32)],
        compiler_params=_cparams("arbitrary", "arbitrary", "arbitrary"),
        name="mla_attention",
    )(q, k, vt)
    ctx_blk = seq // n_ctx
    y_ctx = pl.pallas_call(
        _attn_kernel,
        out_shape=jax.ShapeDtypeStruct((n_batch, n_ctx, nh * V_HEAD), BF16),
        grid=(n_batch, nh),
        in_specs=[
            pl.BlockSpec((None, None, n_ctx, dqk), lambda b, h: (b, h, ctx_blk, 0)),
            pl.BlockSpec((None, None, n_ctx, dqk), lambda b, h: (b, h, ctx_blk, 0)),
            pl.BlockSpec((None, None, V_HEAD, n_ctx), lambda b, h: (b, h, 0, ctx_blk)),
        ],
        out_specs=pl.BlockSpec((None, n_ctx, V_HEAD), lambda b, h: (b, 0, h)),
        scratch_shapes=[pltpu.VMEM((1, n_ctx, n_ctx), F32)],
        compiler_params=_cparams("arbitrary", "arbitrary"),
        name="mla_attention_ctx",
    )(q, k, vt)
    return jnp.concatenate([y_lat, y_ctx], axis=1)


def _hgrn2_tables(tt):
    n = HG_CHUNK
    t = np.arange(n)[:, None]
    r = np.arange(n)[None, :]
    fwd = [(r <= t), (r > t)]
    bwd = [(r >= t), (r < t)]
    for lvl in range(HG_LEVELS):
        m = 1 << lvl
        e = (t >> (lvl + 1)) * 2 * m + m - 1
        right = ((t >> lvl) & 1) == 1
        fwd.append(np.where(right, (r > e) & (r <= t), (r > t) & (r <= e)))
        bwd.append(np.where(right, (r > e) & (r < t), (r >= t) & (r <= e)))
    stack = lambda ms: np.tile(np.concatenate(ms, axis=0).astype(np.float32), (1, 2))
    tq = np.arange(tt)[:, None]
    ts = np.arange(tt)[None, :]
    level = np.floor(np.log2(np.maximum(tq ^ ts, 1))).astype(np.int32)
    lv_f = np.where(ts < tq, level, -1).astype(np.int32)
    return stack(fwd), stack(bwd), lv_f, np.ascontiguousarray(lv_f.T)


def _hgrn2_kernel(qf_ref, ff_ref, vf_ref, qb_ref, fb_ref, vb_ref, lb_ref, mf_ref, mb_ref, lvf_ref,
                  lvb_ref, of_ref, ob_ref, sf_ref, sb_ref, *, heads):
    n = HG_CHUNK
    tt = qf_ref.shape[0]
    wb = qf_ref.shape[1]
    nc = tt // n
    n_levels = HG_LEVELS + int(np.log2(nc))

    @pl.when(pl.program_id(2) == 0)
    def _():
        sf_ref[...] = jnp.zeros_like(sf_ref)
        sb_ref[...] = jnp.zeros_like(sb_ref)

    lb = lb_ref[...]
    row = lax.broadcasted_iota(jnp.int32, (tt, 1), 0)

    def prod(vs):
        return functools.reduce(lambda a, b: a * b, vs) if vs else None

    def by_chunk(pieces):
        return jnp.concatenate(
            [x[c * n:(c + 1) * n] if m is None else x[c * n:(c + 1) * n] * m
             for c, (x, m) in enumerate(pieces)], axis=0)

    def one_direction(q_ref, f_ref, v_ref, m_ref, lv_ref, o_ref, s_ref, backward):
        f = lb + (1.0 - lb) * _sigmoid(f_ref[...])
        g = jnp.log(f)
        g_hi = g.astype(BF16)
        g_lo = (g - g_hi.astype(F32)).astype(BF16)
        lanes = lambda x: jnp.concatenate([x[c * n:(c + 1) * n] for c in range(nc)], axis=1)
        g2 = jnp.concatenate([lanes(g_hi), lanes(g_lo)], axis=0)
        x_all = jnp.exp(_dot(m_ref[...], g2))
        q_all = _silu(q_ref[...].astype(F32))
        k_all = 1.0 - f
        v_all = v_ref[...].astype(F32)
        lv = lv_ref[...]
        outs = []
        for h in range(heads):
            cs = slice(h * HG_DK, (h + 1) * HG_DK)
            q, k, v = q_all[:, cs], k_all[:, cs], v_all[:, cs]
            vb = v.astype(BF16)

            def xblk(r):
                return jnp.concatenate(
                    [x_all[r * n:(r + 1) * n, c * wb + h * HG_DK:c * wb + (h + 1) * HG_DK]
                     for c in range(nc)], axis=0)

            eq_c = xblk(0)
            qe = q * eq_c
            ke = k * xblk(1)
            edge = (lambda c: c * n) if backward else (lambda c: c * n + n - 1)
            tot = [eq_c[edge(c):edge(c) + 1] for c in range(nc)]
            before = lambda c, lo: prod([tot[j] for j in range(lo, c)])
            after = lambda c, hi: prod([tot[j] for j in range(c + 1, hi)])

            a = jnp.zeros((tt, tt), F32)
            for lvl in range(n_levels):
                if lvl < HG_LEVELS:
                    is_q = ((row >> lvl) & 1) == (0 if backward else 1)
                    z = xblk(2 + lvl) * jnp.where(is_q, q, k)
                else:
                    mc = 1 << (lvl - HG_LEVELS)
                    pieces = []
                    for c in range(nc):
                        lo = (c // mc) * mc
                        is_left = ((c // mc) & 1) == 0
                        if is_left == backward:
                            pieces.append((qe, after(c, lo + mc) if backward else before(c, lo)))
                        else:
                            pieces.append((ke, before(c, lo) if backward else after(c, lo + mc)))
                    z = by_chunk(pieces)
                zb = z.astype(BF16)
                a = jnp.where(lv == lvl, _dot_nt(zb, zb), a)

            if backward:
                qe_t = by_chunk([(qe, after(c, nc)) for c in range(nc)])
                ke_t = by_chunk([(ke, before(c, 0)) for c in range(nc)])
            else:
                qe_t = by_chunk([(qe, before(c, 0)) for c in range(nc)])
                ke_t = by_chunk([(ke, after(c, nc)) for c in range(nc)])
            s_t = s_ref[h]
            o = _dot_nt(qe_t.astype(BF16), s_t.astype(BF16)) + _dot(a.astype(BF16), vb)
            o = o + jnp.sum(q * k, axis=-1, keepdims=True) * v
            s_ref[h] = s_t * prod(tot) + _dot_tn(vb, ke_t.astype(BF16))
            outs.append(o)
        o_ref[...] = outs[0] if heads == 1 else jnp.concatenate(outs, axis=1)

    one_direction(qf_ref, ff_ref, vf_ref, mf_ref, lvf_ref, of_ref, sf_ref, False)
    one_direction(qb_ref, fb_ref, vb_ref, mb_ref, lvb_ref, ob_ref, sb_ref, True)


def _hgrn2(p_qig, p_f, lb, *, seq, heads_per_step=2):
    n_batch, t, _ = p_qig.shape
    d_k = lb.shape[-1]
    n_heads = d_k // HG_DK
    hb = heads_per_step if n_heads % heads_per_step == 0 else 1
    wb = hb * HG_DK
    tt = t - seq
    assert seq % tt == 0 and tt % HG_CHUNK == 0
    assert p_qig.shape[2] == 3 * d_k and p_f.shape[2] == 2 * d_k
    assert (tt // HG_CHUNK) & (tt // HG_CHUNK - 1) == 0
    nt = t // tt
    ncol = d_k // wb
    mf, mb, lvf, lvb = _hgrn2_tables(tt)

    def tile_f(s):
        return jnp.where(s == 0, nt - 1, s - 1)

    def tile_b(s):
        return jnp.where(s == 0, nt - 1, nt - 1 - s)

    def col(tile, grp):
        return pl.BlockSpec((None, tt, wb), lambda b, g, s: (b, tile(s), grp * ncol + g))

    const = lambda shape: pl.BlockSpec(shape, lambda b, g, s: (0, 0))
    out_sd = jax.ShapeDtypeStruct((n_batch, t, d_k), F32)
    return pl.pallas_call(
        functools.partial(_hgrn2_kernel, heads=hb),
        out_shape=(out_sd, out_sd),
        grid=(n_batch, ncol, nt),
        in_specs=[col(tile_f, 0), col(tile_f, 0), col(tile_f, 1),
                  col(tile_b, 0), col(tile_b, 1), col(tile_b, 1),
                  pl.BlockSpec((1, wb), lambda b, g, s: (0, g)),
                  const(mf.shape), const(mb.shape), const(lvf.shape), const(lvb.shape)],
        out_specs=(pl.BlockSpec((None, tt, wb), lambda b, g, s: (b, tile_f(s), g)),
                   pl.BlockSpec((None, tt, wb), lambda b, g, s: (b, tile_b(s), g))),
        scratch_shapes=[pltpu.VMEM((hb, HG_DK, HG_DK), F32), pltpu.VMEM((hb, HG_DK, HG_DK), F32)],
        compiler_params=_cparams("arbitrary", "arbitrary", "arbitrary"),
        name="hgrn2_scan",
    )(p_qig, p_f, p_qig, p_qig, p_f, p_qig, lb.reshape(1, d_k), jnp.asarray(mf, BF16),
      jnp.asarray(mb, BF16), jnp.asarray(lvf), jnp.asarray(lvb))


def _outproj_hg_kernel(of_ref, ob_ref, gate_ref, nw_ref, w_ref, x_ref, g_ref, o_ref, y_ref,
                       *, n_batch, seq, t_total):
    tm = x_ref.shape[0]
    nw = nw_ref[...]
    dv = nw.shape[1]
    for h in range(of_ref.shape[1] // dv):
        cs = slice(h * dv, (h + 1) * dv)
        o = of_ref[:, cs] + ob_ref[:, cs]
        y = o * lax.rsqrt(jnp.mean(o * o, axis=-1, keepdims=True) + EPS) * nw
        y_ref[:, cs] = (y * _silu(gate_ref[:, cs].astype(F32))).astype(BF16)
    y = _dot(y_ref[...], w_ref[...])

    def finish(r0, r1, rid):
        o_ref[r0:r1, :] = x_ref[r0:r1, :] + g_ref[pl.ds(rid, 1), :] * y[r0:r1, :]

    _for_row_groups(tm, t_total, seq, n_batch, finish)


def _out_proj_hg(o_f, o_b, p, norm_w, w_out, xs, mods, layer, w_layer, *, seq):
    n_batch, t, d = xs.shape
    d_v = o_f.shape[2]
    tm = _row_tile(t) // 2
    assert tm % 16 == 0
    gate_blk = (p.shape[2] - d_v) // d_v
    kern = functools.partial(_outproj_hg_kernel, n_batch=n_batch, seq=seq, t_total=t)
    rows = lambda width, cblk: pl.BlockSpec((None, tm, width), lambda b, i: (b, i, cblk))
    return pl.pallas_call(
        kern,
        out_shape=jax.ShapeDtypeStruct((n_batch, t, d), F32),
        grid=(n_batch, t // tm),
        in_specs=[
            rows(d_v, 0), rows(d_v, 0), rows(d_v, gate_blk),
            pl.BlockSpec((1, norm_w.shape[0]), lambda b, i: (0, 0)),
            pl.BlockSpec((None, d_v, d), lambda b, i: (w_layer, 0, 0)),
            rows(d, 0),
            _mod_spec(d, 2, layer, 2),
        ],
        out_specs=rows(d, 0),
        scratch_shapes=[pltpu.VMEM((tm, d_v), BF16)],
        compiler_params=_cparams("arbitrary", "arbitrary"),
        name="out_proj_hgrn2",
    )(o_f, o_b, p, norm_w.reshape(1, -1), w_out, xs, mods)


def _rot_columns(w):
    q = QK_ROPE // 4
    return jnp.concatenate([-w[..., q:2 * q], w[..., :q], -w[..., 3 * q:], w[..., 2 * q:3 * q]], axis=-1)


def _rope_tables(seq, n_ctx):
    t = np.arange(seq)
    row, colm = t // GRID_W, t % GRID_W
    half = QK_ROPE // 2
    inv_freq = ROPE_BASE ** (-np.arange(0, half, 2, dtype=np.float32) / half)
    ang_r = row.astype(np.float32)[:, None] * inv_freq
    ang_c = colm.astype(np.float32)[:, None] * inv_freq
    ang = jnp.asarray(np.concatenate([ang_r, ang_r, ang_c, ang_c], axis=-1), F32)
    cos = jnp.concatenate([jnp.cos(ang), jnp.ones((n_ctx, QK_ROPE), F32)], axis=0)
    sin = jnp.concatenate([jnp.sin(ang), jnp.zeros((n_ctx, QK_ROPE), F32)], axis=0)
    return cos, sin


def kernel(x, c, ctx, c_ctx, ada_w, ada_b, norm_mix_w, norm_mlp_w, ab_w_in, ab_w_out, lru_conv_w, lru_conv_b, lru_w_a, lru_b_a, lru_w_x, lru_b_x, lru_lambda, mla_q_norm_w, mla_w_uq, mla_kv_norm_w, mla_w_ukv, hg_w_in, hg_lb_logits, hg_norm_w, hg_w_out, mlp_w1, mlp_w2, final_norm_w):
    n_batch, seq, d = x.shape
    n_ctx = ctx.shape[1]
    depth = ada_w.shape[0]
    assert n_batch + 1 <= SUBLANES

    xs = jnp.concatenate([x, ctx], axis=1)
    cond8 = jnp.zeros((SUBLANES, d), F32).at[:n_batch].set(c).at[n_batch].set(c_ctx)
    mods = _ada_all(cond8, ada_w, ada_b)
    lb_all = _lower_bounds(hg_lb_logits)
    cos_t, sin_t = _rope_tables(seq, n_ctx)

    d_lru = lru_conv_w.shape[-1]
    q_lora = mla_q_norm_w.shape[-1]
    col_kr = ab_w_in.shape[2] - QK_ROPE
    ab_w_in_b = ab_w_in.astype(BF16)
    w_kr = ab_w_in[..., col_kr:]
    w_kr_ext = jnp.concatenate([w_kr, _rot_columns(w_kr)], axis=-1).astype(BF16)
    tn_ab = 1024 if col_kr % 1024 == 0 else col_kr // 3
    assert tn_ab % LANES == 0 and col_kr % tn_ab == 0
    d_hg = hg_lb_logits.shape[-1]
    hg_w_in_b = hg_w_in.astype(BF16)
    tn_hg = 1024 if d_hg % 1024 == 0 else d_hg
    ab_w_out_b = ab_w_out.astype(BF16)
    hg_w_out_b = hg_w_out.astype(BF16)
    w1_b = mlp_w1.astype(BF16)
    w2_b = mlp_w2.astype(BF16)
    wq = mla_w_uq.reshape(-1, q_lora, MLA_HEADS, QK_NOPE + QK_ROPE)
    wq_ext = jnp.concatenate([wq, _rot_columns(wq[..., QK_NOPE:])], axis=-1)
    wq_ext = wq_ext.reshape(wq.shape[0], q_lora, -1).astype(BF16)
    wkv_b = mla_w_ukv.astype(BF16)
    lru_w_a_b = lru_w_a.astype(BF16)
    lru_w_x_b = lru_w_x.astype(BF16)

    for l in range(depth):
        if l % 2 == 0:
            e = l // 2
            p, p_kr = _in_proj(xs, norm_mix_w[l], mods, ab_w_in_b, l, e, seq=seq, tn=tn_ab,
                               n_cols=col_kr, w_extra=w_kr_ext)
            ya = _lru(p, lru_conv_w[e], lru_conv_b[e], lru_w_a_b[e], lru_b_a[e],
                      lru_w_x_b[e], lru_b_x[e], lru_lambda[e], seq=seq)
            q, k, vt = _mla_qkv(p, p_kr, mla_q_norm_w[e], mla_kv_norm_w[e], wq_ext[e], wkv_b[e],
                                cos_t, sin_t, seq=seq, col0=2 * d_lru)
            yb = _attention(q, k, vt, seq=seq)
            xs = _out_proj_ab(ya, yb, ab_w_out_b, xs, mods, l, e, seq=seq)
        else:
            o = l // 2
            p_qig, p_f = _in_proj(xs, norm_mix_w[l], mods, hg_w_in_b, l, o, seq=seq, tn=tn_hg,
                                  f32_cols=(d_hg, 3 * d_hg))
            o_f, o_b = _hgrn2(p_qig, p_f, lb_all[l], seq=seq)
            xs = _out_proj_hg(o_f, o_b, p_qig, hg_norm_w[o], hg_w_out_b, xs, mods, l, o, seq=seq)
        last = l == depth - 1
        xs = _mlp(xs, norm_mlp_w[l], mods, w1_b, w2_b, l, seq=seq,
                  final_norm_w=final_norm_w if last else None)
    return xs
```

```python
import functools

import numpy as np
import jax
import jax.numpy as jnp
from jax import lax
from jax.experimental import pallas as pl
from jax.experimental.pallas import tpu as pltpu

GRID_W = 64
EPS = 1e-6
N_MOD = 6
LRU_HEADS = 8
CONV_W = 4
CONV_LEFT = CONV_W // 2
LRU_C = 8.0
MLA_HEADS = 8
QK_NOPE = 128
QK_ROPE = 64
V_HEAD = 128
ROPE_BASE = 10000.0
HG_DK = 128
HG_CHUNK = 64

LANES = 128
SUBLANES = 8
VMEM_LIMIT_BYTES = 56 * 1024 * 1024
NORM_ROW_CHUNK = 32
ATTN_KEY_CHUNK = 1024
ATTN_Q_TILE = 512
ATTN_TILES_PER_STEP = 4
MLP_LATENT_ROW_TILE = 512

F32 = jnp.float32
BF16 = jnp.bfloat16

HG_LEVELS = int(np.log2(HG_CHUNK))


def _cparams(*sem):
    return pltpu.CompilerParams(dimension_semantics=sem, vmem_limit_bytes=VMEM_LIMIT_BYTES)


def _dot(a, b):
    return jnp.dot(a, b, preferred_element_type=F32)


def _dot_nt(a, b):
    return lax.dot_general(a, b, (((1,), (1,)), ((), ())), preferred_element_type=F32)


def _dot_tn(a, b):
    return lax.dot_general(a, b, (((0,), (0,)), ((), ())), preferred_element_type=F32)


LOG2E = np.float32(np.log2(np.e))


def _sigmoid(x):
    return 1.0 / (1.0 + jnp.exp2(x * (-LOG2E)))


def _silu(x):
    return x * _sigmoid(x)


def _gelu_tanh(x):
    c = np.float32(np.sqrt(2.0 / np.pi))
    return 0.5 * x * (1.0 + jnp.tanh(c * (x + np.float32(0.044715) * (x * x * x))))


def _softplus(x):
    return jnp.maximum(x, 0.0) + jnp.log1p(jnp.exp(-jnp.abs(x)))


def _row_tile(t_total):
    tm = t_total // 8
    assert tm * 8 == t_total and tm % 16 == 0, t_total
    return tm


def _ada_kernel(c_ref, w_ref, b_ref, o_ref):
    a = _silu(c_ref[...]).astype(BF16)
    o_ref[...] = _dot(a, w_ref[...].astype(BF16)) + b_ref[...]


def _ada_all(cond8, ada_w, ada_b):
    n_layers, d, n6 = ada_w.shape
    tn = 1024 if n6 % 1024 == 0 else n6 // N_MOD
    return pl.pallas_call(
        _ada_kernel,
        out_shape=jax.ShapeDtypeStruct((n_layers, SUBLANES, n6), F32),
        grid=(n_layers, n6 // tn),
        in_specs=[
            pl.BlockSpec((SUBLANES, d), lambda l, j: (0, 0)),
            pl.BlockSpec((None, d, tn), lambda l, j: (l, 0, j)),
            pl.BlockSpec((None, 1, tn), lambda l, j: (l, 0, j)),
        ],
        out_specs=pl.BlockSpec((None, SUBLANES, tn), lambda l, j: (l, 0, j)),
        compiler_params=_cparams("arbitrary", "arbitrary"),
        name="ada_params",
    )(cond8, ada_w, ada_b.reshape(n_layers, 1, n6))


def _lb_kernel(x_ref, o_ref):
    n = x_ref.shape[0]
    rows = [x_ref[pl.ds(l, 1), :] for l in range(n)]
    mx = functools.reduce(jnp.maximum, rows)
    es = [jnp.exp(r - mx) for r in rows]
    tot = functools.reduce(lambda a, b: a + b, es)
    ps = [e / tot for e in es]
    acc = ps[0]
    o_ref[pl.ds(0, 1), :] = acc - ps[0]
    for l in range(1, n):
        acc = acc + ps[l]
        o_ref[pl.ds(l, 1), :] = acc - ps[0]


def _lower_bounds(logits):
    return pl.pallas_call(
        _lb_kernel, out_shape=jax.ShapeDtypeStruct(logits.shape, F32), name="hgrn2_lower_bounds"
    )(logits.astype(F32))


def _for_row_groups(tm, t_total, seq, n_batch, fn):
    b = pl.program_id(0)
    i = pl.program_id(1)
    i_mixed, split = divmod(seq, tm)
    if i_mixed > 0:
        pl.when(i < i_mixed)(lambda: fn(0, tm, b))
    if i_mixed < t_total // tm:
        @pl.when(i == i_mixed)
        def _():
            if split > 0:
                fn(0, split, b)
            fn(split, tm, n_batch)
    if i_mixed + 1 < t_total // tm:
        pl.when(i > i_mixed)(lambda: fn(0, tm, n_batch))


def _norm_modulate_rows(x_ref, nw_ref, sh_ref, sc_ref, h_ref, r0, r1, rid):
    gain = nw_ref[...] * (1.0 + sc_ref[pl.ds(rid, 1), :])
    shift = sh_ref[pl.ds(rid, 1), :]
    rc = NORM_ROW_CHUNK
    for r in range(r0, r1, rc):
        x = x_ref[r:min(r + rc, r1), :]
        ms = jnp.mean(x * x, axis=-1, keepdims=True)
        h_ref[r:min(r + rc, r1), :] = (x * lax.rsqrt(ms + EPS) * gain + shift).astype(h_ref.dtype)


def _mod_spec(d, k, layer, nargs):
    if nargs == 2:
        return pl.BlockSpec((None, SUBLANES, d), lambda b, i: (layer, 0, k))
    return pl.BlockSpec((None, SUBLANES, d), lambda b, i, j: (layer, 0, k))


def _inproj_kernel(*refs, n_batch, seq, t_total, f32_tiles, has_extra):
    x_ref, nw_ref, sh_ref, sc_ref, w_ref = refs[:5]
    h_ref = refs[-1]
    rest = list(refs[5:-1])
    wx_ref = rest.pop(0) if has_extra else None
    lo_ref = rest.pop(0)
    hi_ref = rest.pop(0) if f32_tiles else None
    ox_ref = rest.pop(0) if has_extra else None
    tm = x_ref.shape[0]
    j = pl.program_id(2)

    @pl.when(j == 0)
    def _():
        _for_row_groups(tm, t_total, seq, n_batch, functools.partial(
            _norm_modulate_rows, x_ref, nw_ref, sh_ref, sc_ref, h_ref))
        if has_extra:
            ox_ref[...] = _dot(h_ref[...], wx_ref[...]).astype(ox_ref.dtype)

    y = _dot(h_ref[...], w_ref[...])
    if not f32_tiles:
        lo_ref[...] = y.astype(lo_ref.dtype)
    else:
        is_f32 = jnp.logical_and(j >= f32_tiles[0], j < f32_tiles[1])

        @pl.when(jnp.logical_not(is_f32))
        def _():
            lo_ref[...] = y.astype(lo_ref.dtype)

        @pl.when(is_f32)
        def _():
            hi_ref[...] = y.astype(hi_ref.dtype)


def _in_proj(xs, norm_w, mods, w, layer, w_layer, *, seq, tn, n_cols=None, f32_cols=None, w_extra=None):
    n_batch, t, d = xs.shape
    n_out = w.shape[2] if n_cols is None else n_cols
    tm = _row_tile(t) * 2
    assert n_out % tn == 0
    n_tiles = n_out // tn
    j0, j1 = (0, 0) if f32_cols is None else (f32_cols[0] // tn, f32_cols[1] // tn)
    assert f32_cols is None or (f32_cols[0] % tn == 0 and f32_cols[1] % tn == 0 and j0 < j1 <= n_tiles)
    n_hi = j1 - j0
    kern = functools.partial(_inproj_kernel, n_batch=n_batch, seq=seq, t_total=t,
                             f32_tiles=(j0, j1) if n_hi else None, has_extra=w_extra is not None)

    def lo_idx(j):
        return jnp.where(j < j0, j, jnp.where(j < j1, max(j0 - 1, 0), j - n_hi))

    in_specs = [
        pl.BlockSpec((None, tm, d), lambda b, i, j: (b, i, 0)),
        pl.BlockSpec((1, d), lambda b, i, j: (0, 0)),
        _mod_spec(d, 0, layer, 3),
        _mod_spec(d, 1, layer, 3),
        pl.BlockSpec((None, d, tn), lambda b, i, j: (w_layer, 0, j)),
    ]
    args = [xs, norm_w.reshape(1, d), mods, mods, w]
    out_shape = [jax.ShapeDtypeStruct((n_batch, t, (n_tiles - n_hi) * tn), BF16)]
    out_specs = [pl.BlockSpec((None, tm, tn), lambda b, i, j: (b, i, lo_idx(j)))]
    if n_hi:
        out_shape.append(jax.ShapeDtypeStruct((n_batch, t, n_hi * tn), F32))
        out_specs.append(pl.BlockSpec((None, tm, tn), lambda b, i, j: (b, i, jnp.clip(j - j0, 0, n_hi - 1))))
    if w_extra is not None:
        in_specs.append(pl.BlockSpec((None, d, LANES), lambda b, i, j: (w_layer, 0, 0)))
        args.append(w_extra)
        out_shape.append(jax.ShapeDtypeStruct((n_batch, t, LANES), BF16))
        out_specs.append(pl.BlockSpec((None, tm, LANES), lambda b, i, j: (b, i, 0)))
    outs = pl.pallas_call(
        kern,
        out_shape=out_shape,
        grid=(n_batch, t // tm, n_tiles),
        in_specs=in_specs,
        out_specs=out_specs,
        scratch_shapes=[pltpu.VMEM((tm, d), BF16)],
        compiler_params=_cparams("arbitrary", "arbitrary", "arbitrary"),
        name="in_proj",
    )(*args)
    return outs if len(outs) > 1 else outs[0]


def _mlp_kernel(x_ref, nw_ref, sh_ref, sc_ref, g_ref, w1_ref, w2_ref, *rest, n_batch, seq, t_total):
    fw_ref = rest[0] if len(rest) == 4 else None
    o_ref, h_ref, acc_ref = rest[-3:]
    f = pl.program_id(2)
    tm = x_ref.shape[0]

    @pl.when(f == 0)
    def _():
        _for_row_groups(tm, t_total, seq, n_batch, functools.partial(
            _norm_modulate_rows, x_ref, nw_ref, sh_ref, sc_ref, h_ref))
        acc_ref[...] = jnp.zeros_like(acc_ref)

    a = jnp.maximum(_dot(h_ref[...], w1_ref[...]), 0.0)
    acc_ref[...] += _dot((a * a).astype(BF16), w2_ref[...])

    @pl.when(f == pl.num_programs(2) - 1)
    def _():
        def finish(r0, r1, rid):
            rc = NORM_ROW_CHUNK if fw_ref is not None else r1 - r0
            for r in range(r0, r1, rc):
                rs = slice(r, min(r + rc, r1))
                y = x_ref[rs, :] + g_ref[pl.ds(rid, 1), :] * acc_ref[rs, :]
                if fw_ref is not None:
                    y = y * lax.rsqrt(jnp.mean(y * y, axis=-1, keepdims=True) + EPS) * fw_ref[...]
                o_ref[rs, :] = y

        _for_row_groups(tm, t_total, seq, n_batch, finish)


def _mlp(xs, norm_w, mods, w1, w2, layer, *, seq, final_norm_w=None):
    n_batch, t, d = xs.shape
    d_ff = w1.shape[2]
    if final_norm_w is None:
        rows, tm = t, _row_tile(t)
    else:
        rows = seq
        tm = MLP_LATENT_ROW_TILE if seq % MLP_LATENT_ROW_TILE == 0 else t - seq
        assert seq % tm == 0
    tf = 1024 if d_ff % 1024 == 0 else d_ff
    kern = functools.partial(_mlp_kernel, n_batch=n_batch, seq=seq, t_total=rows)
    in_specs = [
        pl.BlockSpec((None, tm, d), lambda b, i, f: (b, i, 0)),
        pl.BlockSpec((1, d), lambda b, i, f: (0, 0)),
        _mod_spec(d, 3, layer, 3),
        _mod_spec(d, 4, layer, 3),
        _mod_spec(d, 5, layer, 3),
        pl.BlockSpec((None, d, tf), lambda b, i, f: (layer, 0, f)),
        pl.BlockSpec((None, tf, d), lambda b, i, f: (layer, f, 0)),
    ]
    args = [xs, norm_w.reshape(1, d), mods, mods, mods, w1, w2]
    if final_norm_w is not None:
        in_specs.append(pl.BlockSpec((1, d), lambda b, i, f: (0, 0)))
        args.append(final_norm_w.reshape(1, d))
    return pl.pallas_call(
        kern,
        out_shape=jax.ShapeDtypeStruct((n_batch, rows, d), F32),
        grid=(n_batch, rows // tm, d_ff // tf),
        in_specs=in_specs,
        out_specs=pl.BlockSpec((None, tm, d), lambda b, i, f: (b, i, 0)),
        scratch_shapes=[pltpu.VMEM((tm, d), BF16), pltpu.VMEM((tm, d), F32)],
        compiler_params=_cparams("arbitrary", "arbitrary", "arbitrary"),
        name="mlp",
    )(*args)


def _outproj_ab_kernel(ya_ref, yb_ref, w_ref, x_ref, g_ref, o_ref, *, n_batch, seq, t_total):
    tm = x_ref.shape[0]
    da = ya_ref.shape[1]
    y = _dot(ya_ref[...], w_ref[0:da, :]) + _dot(yb_ref[...], w_ref[da:, :])

    def finish(r0, r1, rid):
        o_ref[r0:r1, :] = x_ref[r0:r1, :] + g_ref[pl.ds(rid, 1), :] * y[r0:r1, :]

    _for_row_groups(tm, t_total, seq, n_batch, finish)


def _out_proj_ab(ya, yb, w_out, xs, mods, layer, w_layer, *, seq):
    n_batch, t, d = xs.shape
    da, db = ya.shape[2], yb.shape[2]
    tm = _row_tile(t)
    kern = functools.partial(_outproj_ab_kernel, n_batch=n_batch, seq=seq, t_total=t)
    return pl.pallas_call(
        kern,
        out_shape=jax.ShapeDtypeStruct((n_batch, t, d), F32),
        grid=(n_batch, t // tm),
        in_specs=[
            pl.BlockSpec((None, tm, da), lambda b, i: (b, i, 0)),
            pl.BlockSpec((None, tm, db), lambda b, i: (b, i, 0)),
            pl.BlockSpec((None, da + db, d), lambda b, i: (w_layer, 0, 0)),
            pl.BlockSpec((None, tm, d), lambda b, i: (b, i, 0)),
            _mod_spec(d, 2, layer, 2),
        ],
        out_specs=pl.BlockSpec((None, tm, d), lambda b, i: (b, i, 0)),
        compiler_params=_cparams("arbitrary", "arbitrary"),
        name="out_proj_ab",
    )(ya, yb, w_out, xs, mods)


def _lru_kernel(gate_ref, u_ref, cw_ref, cb_ref, wa_ref, ba_ref, wx_ref, bx_ref, lam_ref, o_ref,
                upad_ref, af_ref, bf_ref, ab_ref, bb_ref, hf_ref, hb_ref, *, seq, chunk):
    t_total = u_ref.shape[0]
    n_ctx = t_total - seq
    pad = SUBLANES
    cw = cw_ref[...]
    cb = cb_ref[...]
    log2_a_unit = _softplus(-lam_ref[...]) * (-LRU_C * LOG2E)
    zeros_pad = jnp.zeros((pad, LANES), F32)

    def gates(uc, d):
        ucb = uc.astype(BF16)
        r = _sigmoid(_dot(ucb, wa_ref[d]) + ba_ref[pl.ds(d, 1), :])
        i = _sigmoid(_dot(ucb, wx_ref[d]) + bx_ref[pl.ds(d, 1), :])
        a = jnp.exp2(r * log2_a_unit[d:d + 1, :])
        return a, jnp.sqrt(1.0 - a * a) * (i * uc)

    def prepare(src0, dst0, n):
        upad_ref[pl.ds(0, pad), :] = zeros_pad
        upad_ref[pl.ds(pad + n, pad), :] = zeros_pad
        upad_ref[pl.ds(pad, n), :] = u_ref[pl.ds(src0, n), :].astype(F32)
        ch = min(chunk, n)

        def body(c, carry):
            r0 = pl.multiple_of(c * ch, SUBLANES)
            win = upad_ref[pl.ds(r0, ch + 2 * pad), :]
            uc = cb
            for j in range(CONV_W):
                off = pad + j - CONV_LEFT
                uc = uc + win[off:off + ch, :] * cw[j:j + 1, :]
            a0, b0 = gates(uc, 0)
            a1, b1 = gates(uc, 1)
            d0 = pl.multiple_of(dst0 + r0, SUBLANES)
            af_ref[pl.ds(d0, ch), :] = a0
            bf_ref[pl.ds(d0, ch), :] = b0
            ab_ref[pl.ds(d0, ch), :] = a1
            bb_ref[pl.ds(d0, ch), :] = b1
            return carry

        lax.fori_loop(0, n // ch, body, 0)

    prepare(seq, 0, n_ctx)
    prepare(0, n_ctx, seq)

    row = lax.broadcasted_iota(jnp.int32, (SUBLANES, LANES), 0)

    def scan_fwd(a, b):
        for s in (1, 2, 4):
            m = row >= s
            a_s = pltpu.roll(a, s, 0)
            b_s = pltpu.roll(b, s, 0)
            b = jnp.where(m, a * b_s + b, b)
            a = jnp.where(m, a * a_s, a)
        return a, b

    def scan_bwd(a, b):
        for s in (1, 2, 4):
            m = row < SUBLANES - s
            a_s = pltpu.roll(a, SUBLANES - s, 0)
            b_s = pltpu.roll(b, SUBLANES - s, 0)
            b = jnp.where(m, a * b_s + b, b)
            a = jnp.where(m, a * a_s, a)
        return a, b

    def scan_range(g0, n_groups, carry):
        def body(g, c):
            cf, cbk = c
            rf = pl.multiple_of((g0 + g) * SUBLANES, SUBLANES)
            a, b = scan_fwd(af_ref[pl.ds(rf, SUBLANES), :], bf_ref[pl.ds(rf, SUBLANES), :])
            hf = a * cf + b
            hf_ref[pl.ds(rf, SUBLANES), :] = hf
            rb = pl.multiple_of((g0 + n_groups - 1 - g) * SUBLANES, SUBLANES)
            a, b = scan_bwd(ab_ref[pl.ds(rb, SUBLANES), :], bb_ref[pl.ds(rb, SUBLANES), :])
            hb = a * cbk + b
            hb_ref[pl.ds(rb, SUBLANES), :] = hb
            cf = jnp.broadcast_to(hf[SUBLANES - 1:SUBLANES, :], (SUBLANES, LANES))
            cbk = jnp.broadcast_to(hb[0:1, :], (SUBLANES, LANES))
            return cf, cbk

        return lax.fori_loop(0, n_groups, body, carry, unroll=4)

    zero = jnp.zeros((SUBLANES, LANES), F32)
    carry = scan_range(0, n_ctx // SUBLANES, (zero, zero))
    scan_range(n_ctx // SUBLANES, seq // SUBLANES, carry)

    def finish(src0, dst0, n):
        ch = min(chunk, n)

        def body(c, carry):
            r0 = c * ch
            s0 = pl.multiple_of(src0 + r0, SUBLANES)
            d0 = pl.multiple_of(dst0 + r0, SUBLANES)
            h = hf_ref[pl.ds(s0, ch), :] + hb_ref[pl.ds(s0, ch), :]
            gate = gate_ref[pl.ds(d0, ch), :].astype(F32)
            o_ref[pl.ds(d0, ch), :] = (h * _gelu_tanh(gate)).astype(o_ref.dtype)
            return carry

        lax.fori_loop(0, n // ch, body, 0)

    finish(0, seq, n_ctx)
    finish(n_ctx, 0, seq)


def _lru(p, conv_w, conv_b, w_a, b_a, w_x, b_x, lam, *, seq):
    n_batch, t, _ = p.shape
    d_lru = conv_w.shape[1]
    n_heads = w_a.shape[1]
    assert d_lru // n_heads == LANES
    n_ctx = t - seq
    chunk = 256
    assert seq % min(chunk, seq) == 0 and n_ctx % min(chunk, n_ctx) == 0
    kern = functools.partial(_lru_kernel, seq=seq, chunk=chunk)
    vec = lambda rows: pl.BlockSpec((rows, LANES), lambda b, h: (0, h))
    wspec = pl.BlockSpec((2, None, LANES, LANES), lambda b, h: (0, h, 0, 0))
    scratch_rows = pltpu.VMEM((t, LANES), F32)
    return pl.pallas_call(
        kern,
        out_shape=jax.ShapeDtypeStruct((n_batch, t, d_lru), BF16),
        grid=(n_batch, n_heads),
        in_specs=[
            pl.BlockSpec((None, t, LANES), lambda b, h: (b, 0, h)),
            pl.BlockSpec((None, t, LANES), lambda b, h: (b, 0, n_heads + h)),
            vec(CONV_W), vec(1), wspec, vec(2), wspec, vec(2), vec(2),
        ],
        out_specs=pl.BlockSpec((None, t, LANES), lambda b, h: (b, 0, h)),
        scratch_shapes=[pltpu.VMEM((max(seq, n_ctx) + 2 * SUBLANES, LANES), F32),
                        scratch_rows, scratch_rows, scratch_rows, scratch_rows,
                        scratch_rows, scratch_rows],
        compiler_params=_cparams("arbitrary", "arbitrary"),
        name="rglru",
    )(p, p, conv_w, conv_b.reshape(1, d_lru), w_a, b_a, w_x, b_x, lam)


def _mla_qkv_kernel(cq_ref, ckv_ref, kr_ref, qn_ref, kvn_ref, wq_ref, wkv_ref, cos_ref, sin_ref,
                    q_ref, k_ref, v_ref, *, scale):
    def rms(x, w):
        return (x * lax.rsqrt(jnp.mean(x * x, axis=-1, keepdims=True) + EPS) * w).astype(BF16)

    cos = cos_ref[...]
    sin = sin_ref[...]
    q_all = _dot(rms(cq_ref[...].astype(F32), qn_ref[...]), wq_ref[...])
    kv_all = _dot(rms(ckv_ref[...].astype(F32), kvn_ref[...]), wkv_ref[...])
    kr = kr_ref[...].astype(F32)
    kr_rot = kr[:, :QK_ROPE] * cos + kr[:, QK_ROPE:] * sin
    wq_head = QK_NOPE + 2 * QK_ROPE
    wkv_head = QK_NOPE + V_HEAD
    for h in range(q_ref.shape[0]):
        qh = q_all[:, h * wq_head:(h + 1) * wq_head]
        qr = qh[:, QK_NOPE:QK_NOPE + QK_ROPE] * cos + qh[:, QK_NOPE + QK_ROPE:] * sin
        q_ref[h, :, 0:QK_NOPE] = (qh[:, :QK_NOPE] * scale).astype(BF16)
        q_ref[h, :, QK_NOPE:] = (qr * scale).astype(BF16)
        kvh = kv_all[:, h * wkv_head:(h + 1) * wkv_head]
        k_ref[h, :, 0:QK_NOPE] = kvh[:, :QK_NOPE].astype(BF16)
        k_ref[h, :, QK_NOPE:] = kr_rot.astype(BF16)
        v_ref[h] = kvh[:, QK_NOPE:].T.astype(BF16)


def _mla_qkv(p, p_kr, q_norm_w, kv_norm_w, wq_ext, wkv, cos_t, sin_t, *, seq, col0):
    n_batch, t, _ = p.shape
    q_lora = q_norm_w.shape[0]
    kv_lora = kv_norm_w.shape[0]
    assert q_lora == kv_lora and col0 % q_lora == 0 and p_kr.shape[2] == 2 * QK_ROPE == LANES
    n_ctx = t - seq
    tm = n_ctx
    nh = MLA_HEADS
    dqk = QK_NOPE + QK_ROPE
    scale = np.float32(dqk ** -0.5 * np.log2(np.e))
    cq_blk = col0 // q_lora
    full = lambda shape: pl.BlockSpec(shape, lambda b, i: (0,) * len(shape))
    return pl.pallas_call(
        functools.partial(_mla_qkv_kernel, scale=scale),
        out_shape=(jax.ShapeDtypeStruct((n_batch, nh, t, dqk), BF16),
                   jax.ShapeDtypeStruct((n_batch, nh, t, dqk), BF16),
                   jax.ShapeDtypeStruct((n_batch, nh, V_HEAD, t), BF16)),
        grid=(n_batch, t // tm),
        in_specs=[
            pl.BlockSpec((None, tm, q_lora), lambda b, i: (b, i, cq_blk)),
            pl.BlockSpec((None, tm, kv_lora), lambda b, i: (b, i, cq_blk + 1)),
            pl.BlockSpec((None, tm, LANES), lambda b, i: (b, i, 0)),
            full((1, q_lora)), full((1, kv_lora)), full(wq_ext.shape), full(wkv.shape),
            pl.BlockSpec((tm, QK_ROPE), lambda b, i: (i, 0)),
            pl.BlockSpec((tm, QK_ROPE), lambda b, i: (i, 0)),
        ],
        out_specs=(pl.BlockSpec((None, nh, tm, dqk), lambda b, i: (b, 0, i, 0)),
                   pl.BlockSpec((None, nh, tm, dqk), lambda b, i: (b, 0, i, 0)),
                   pl.BlockSpec((None, nh, V_HEAD, tm), lambda b, i: (b, 0, 0, i))),
        compiler_params=_cparams("arbitrary", "arbitrary"),
        name="mla_qkv",
    )(p, p, p_kr, q_norm_w.reshape(1, q_lora), kv_norm_w.reshape(1, kv_lora), wq_ext, wkv, cos_t, sin_t)


def _softmax_values(qs, k_ref, vt_ref, s_ref):
    t = k_ref.shape[0]
    bounds = list(range(0, t, ATTN_KEY_CHUNK)) + [t]
    chunks = list(zip(bounds[:-1], bounds[1:]))
    n = len(qs)
    ms, ls, outs = [None] * n, [None] * n, [None] * n
    for stage in range(n + 1):
        for a, b in chunks:
            if stage < n:
                s = _dot_nt(k_ref[a:b, :], qs[stage])
                s_ref[stage, a:b, :] = s
                cm = jnp.max(s, axis=0, keepdims=True)
                ms[stage] = cm if ms[stage] is None else jnp.maximum(ms[stage], cm)
            if stage >= 1:
                i = stage - 1
                p = jnp.exp2(s_ref[i, a:b, :] - ms[i])
                ps = jnp.sum(p, axis=0, keepdims=True)
                pv = _dot(vt_ref[:, a:b], p.astype(BF16))
                ls[i] = ps if ls[i] is None else ls[i] + ps
                outs[i] = pv if outs[i] is None else outs[i] + pv
    return [(outs[i] * (1.0 / ls[i])).T for i in range(n)]


def _attn_kernel(q_ref, k_ref, vt_ref, o_ref, s_ref):
    n = s_ref.shape[0]
    tq = q_ref.shape[0] // n
    outs = _softmax_values([q_ref[i * tq:(i + 1) * tq, :] for i in range(n)], k_ref, vt_ref, s_ref)
    for i in range(n):
        o_ref[i * tq:(i + 1) * tq, :] = outs[i].astype(o_ref.dtype)


def _attention(q, k, vt, *, seq):
    n_batch, nh, t, dqk = q.shape
    n_ctx = t - seq
    tq = ATTN_Q_TILE if seq % ATTN_Q_TILE == 0 else n_ctx
    assert seq % tq == 0 and seq % n_ctx == 0
    nq = ATTN_TILES_PER_STEP if seq % (ATTN_TILES_PER_STEP * tq) == 0 else 1
    y_lat = pl.pallas_call(
        _attn_kernel,
        out_shape=jax.ShapeDtypeStruct((n_batch, seq, nh * V_HEAD), BF16),
        grid=(n_batch, nh, seq // (nq * tq)),
        in_specs=[
            pl.BlockSpec((None, None, nq * tq, dqk), lambda b, h, i: (b, h, i, 0)),
            pl.BlockSpec((None, None, t, dqk), lambda b, h, i: (b, h, 0, 0)),
            pl.BlockSpec((None, None, V_HEAD, t), lambda b, h, i: (b, h, 0, 0)),
        ],
        out_specs=pl.BlockSpec((None, nq * tq, V_HEAD), lambda b, h, i: (b, i, h)),
        scratch_shapes=[pltpu.VMEM((nq, t, tq), F32)],
        compiler_params=_cparams("arbitrary", "arbitrary", "arbitrary"),
        name="mla_attention",
    )(q, k, vt)
    ctx_blk = seq // n_ctx
    y_ctx = pl.pallas_call(
        _attn_kernel,
        out_shape=jax.ShapeDtypeStruct((n_batch, n_ctx, nh * V_HEAD), BF16),
        grid=(n_batch, nh),
        in_specs=[
            pl.BlockSpec((None, None, n_ctx, dqk), lambda b, h: (b, h, ctx_blk, 0)),
            pl.BlockSpec((None, None, n_ctx, dqk), lambda b, h: (b, h, ctx_blk, 0)),
            pl.BlockSpec((None, None, V_HEAD, n_ctx), lambda b, h: (b, h, 0, ctx_blk)),
        ],
        out_specs=pl.BlockSpec((None, n_ctx, V_HEAD), lambda b, h: (b, 0, h)),
        scratch_shapes=[pltpu.VMEM((1, n_ctx, n_ctx), F32)],
        compiler_params=_cparams("arbitrary", "arbitrary"),
        name="mla_attention_ctx",
    )(q, k, vt)
    return jnp.concatenate([y_lat, y_ctx], axis=1)


def _hgrn2_tables(tt):
    n = HG_CHUNK
    t = np.arange(n)[:, None]
    r = np.arange(n)[None, :]
    fwd = [(r <= t), (r > t)]
    bwd = [(r >= t), (r < t)]
    for lvl in range(HG_LEVELS):
        m = 1 << lvl
        e = (t >> (lvl + 1)) * 2 * m + m - 1
        right = ((t >> lvl) & 1) == 1
        fwd.append(np.where(right, (r > e) & (r <= t), (r > t) & (r <= e)))
        bwd.append(np.where(right, (r > e) & (r < t), (r >= t) & (r <= e)))
    stack = lambda ms: np.tile(np.concatenate(ms, axis=0).astype(np.float32), (1, 2))
    tq = np.arange(tt)[:, None]
    ts = np.arange(tt)[None, :]
    level = np.floor(np.log2(np.maximum(tq ^ ts, 1))).astype(np.int32)
    lv_f = np.where(ts < tq, level, -1).astype(np.int32)
    return stack(fwd), stack(bwd), lv_f, np.ascontiguousarray(lv_f.T)


def _hgrn2_kernel(qf_ref, ff_ref, vf_ref, qb_ref, fb_ref, vb_ref, lb_ref, mf_ref, mb_ref, lvf_ref,
                  lvb_ref, of_ref, ob_ref, sf_ref, sb_ref, *, heads):
    n = HG_CHUNK
    tt = qf_ref.shape[0]
    wb = qf_ref.shape[1]
    nc = tt // n
    n_levels = HG_LEVELS + int(np.log2(nc))

    @pl.when(pl.program_id(2) == 0)
    def _():
        sf_ref[...] = jnp.zeros_like(sf_ref)
        sb_ref[...] = jnp.zeros_like(sb_ref)

    lb = lb_ref[...]
    row = lax.broadcasted_iota(jnp.int32, (tt, 1), 0)

    def prod(vs):
        return functools.reduce(lambda a, b: a * b, vs) if vs else None

    def by_chunk(pieces):
        return jnp.concatenate(
            [x[c * n:(c + 1) * n] if m is None else x[c * n:(c + 1) * n] * m
             for c, (x, m) in enumerate(pieces)], axis=0)

    def one_direction(q_ref, f_ref, v_ref, m_ref, lv_ref, o_ref, s_ref, backward):
        f = lb + (1.0 - lb) * _sigmoid(f_ref[...])
        g = jnp.log(f) * LOG2E
        g_hi = g.astype(BF16)
        g_lo = (g - g_hi.astype(F32)).astype(BF16)
        lanes = lambda x: jnp.concatenate([x[c * n:(c + 1) * n] for c in range(nc)], axis=1)
        g2 = jnp.concatenate([lanes(g_hi), lanes(g_lo)], axis=0)
        x_all = jnp.exp2(_dot(m_ref[...], g2))
        q_all = _silu(q_ref[...].astype(F32))
        k_all = 1.0 - f
        v_all = v_ref[...].astype(F32)
        lv = lv_ref[...]
        outs = []
        for h in range(heads):
            cs = slice(h * HG_DK, (h + 1) * HG_DK)
            q, k, v = q_all[:, cs], k_all[:, cs], v_all[:, cs]
            vb = v.astype(BF16)

            def xblk(r):
                return jnp.concatenate(
                    [x_all[r * n:(r + 1) * n, c * wb + h * HG_DK:c * wb + (h + 1) * HG_DK]
                     for c in range(nc)], axis=0)

            eq_c = xblk(0)
            qe = q * eq_c
            ke = k * xblk(1)
            edge = (lambda c: c * n) if backward else (lambda c: c * n + n - 1)
            tot = [eq_c[edge(c):edge(c) + 1] for c in range(nc)]
            before = lambda c, lo: prod([tot[j] for j in range(lo, c)])
            after = lambda c, hi: prod([tot[j] for j in range(c + 1, hi)])

            a = jnp.zeros((tt, tt), F32)
            for lvl in range(n_levels):
                if lvl < HG_LEVELS:
                    is_q = ((row >> lvl) & 1) == (0 if backward else 1)
                    z = xblk(2 + lvl) * jnp.where(is_q, q, k)
                else:
                    mc = 1 << (lvl - HG_LEVELS)
                    pieces = []
                    for c in range(nc):
                        lo = (c // mc) * mc
                        is_left = ((c // mc) & 1) == 0
                        if is_left == backward:
                            pieces.append((qe, after(c, lo + mc) if backward else before(c, lo)))
                        else:
                            pieces.append((ke, before(c, lo) if backward else after(c, lo + mc)))
                    z = by_chunk(pieces)
                zb = z.astype(BF16)
                a = jnp.where(lv == lvl, _dot_nt(zb, zb), a)

            if backward:
                qe_t = by_chunk([(qe, after(c, nc)) for c in range(nc)])
                ke_t = by_chunk([(ke, before(c, 0)) for c in range(nc)])
            else:
                qe_t = by_chunk([(qe, before(c, 0)) for c in range(nc)])
                ke_t = by_chunk([(ke, after(c, nc)) for c in range(nc)])
            s_t = s_ref[h]
            o = _dot_nt(qe_t.astype(BF16), s_t.astype(BF16)) + _dot(a.astype(BF16), vb)
            o = o + jnp.sum(q * k, axis=-1, keepdims=True) * v
            s_ref[h] = s_t * prod(tot) + _dot_tn(vb, ke_t.astype(BF16))
            outs.append(o)
        o_all = outs[0] if heads == 1 else jnp.concatenate(outs, axis=1)
        o_ref[...] = o_all.astype(o_ref.dtype)

    one_direction(qf_ref, ff_ref, vf_ref, mf_ref, lvf_ref, of_ref, sf_ref, False)
    one_direction(qb_ref, fb_ref, vb_ref, mb_ref, lvb_ref, ob_ref, sb_ref, True)


def _hgrn2(p_qig, p_f, lb, *, seq, heads_per_step=2):
    n_batch, t, _ = p_qig.shape
    d_k = lb.shape[-1]
    n_heads = d_k // HG_DK
    hb = heads_per_step if n_heads % heads_per_step == 0 else 1
    wb = hb * HG_DK
    tt = t - seq
    assert seq % tt == 0 and tt % HG_CHUNK == 0
    assert p_qig.shape[2] == 3 * d_k and p_f.shape[2] == 2 * d_k
    assert (tt // HG_CHUNK) & (tt // HG_CHUNK - 1) == 0
    nt = t // tt
    ncol = d_k // wb
    mf, mb, lvf, lvb = _hgrn2_tables(tt)

    def tile_f(s):
        return jnp.where(s == 0, nt - 1, s - 1)

    def tile_b(s):
        return jnp.where(s == 0, nt - 1, nt - 1 - s)

    def col(tile, grp):
        return pl.BlockSpec((None, tt, wb), lambda b, g, s: (b, tile(s), grp * ncol + g))

    const = lambda shape: pl.BlockSpec(shape, lambda b, g, s: (0, 0))
    out_sd = jax.ShapeDtypeStruct((n_batch, t, d_k), BF16)
    return pl.pallas_call(
        functools.partial(_hgrn2_kernel, heads=hb),
        out_shape=(out_sd, out_sd),
        grid=(n_batch, ncol, nt),
        in_specs=[col(tile_f, 0), col(tile_f, 0), col(tile_f, 1),
                  col(tile_b, 0), col(tile_b, 1), col(tile_b, 1),
                  pl.BlockSpec((1, wb), lambda b, g, s: (0, g)),
                  const(mf.shape), const(mb.shape), const(lvf.shape), const(lvb.shape)],
        out_specs=(pl.BlockSpec((None, tt, wb), lambda b, g, s: (b, tile_f(s), g)),
                   pl.BlockSpec((None, tt, wb), lambda b, g, s: (b, tile_b(s), g))),
        scratch_shapes=[pltpu.VMEM((hb, HG_DK, HG_DK), F32), pltpu.VMEM((hb, HG_DK, HG_DK), F32)],
        compiler_params=_cparams("arbitrary", "arbitrary", "arbitrary"),
        name="hgrn2_scan",
    )(p_qig, p_f, p_qig, p_qig, p_f, p_qig, lb.reshape(1, d_k), jnp.asarray(mf, BF16),
      jnp.asarray(mb, BF16), jnp.asarray(lvf), jnp.asarray(lvb))


def _outproj_hg_kernel(of_ref, ob_ref, gate_ref, nw_ref, w_ref, x_ref, g_ref, o_ref, y_ref,
                       *, n_batch, seq, t_total):
    tm = x_ref.shape[0]
    nw = nw_ref[...]
    dv = nw.shape[1]
    for h in range(of_ref.shape[1] // dv):
        cs = slice(h * dv, (h + 1) * dv)
        o = of_ref[:, cs].astype(F32) + ob_ref[:, cs].astype(F32)
        y = o * lax.rsqrt(jnp.mean(o * o, axis=-1, keepdims=True) + EPS) * nw
        y_ref[:, cs] = (y * _silu(gate_ref[:, cs].astype(F32))).astype(BF16)
    y = _dot(y_ref[...], w_ref[...])

    def finish(r0, r1, rid):
        o_ref[r0:r1, :] = x_ref[r0:r1, :] + g_ref[pl.ds(rid, 1), :] * y[r0:r1, :]

    _for_row_groups(tm, t_total, seq, n_batch, finish)


def _out_proj_hg(o_f, o_b, p, norm_w, w_out, xs, mods, layer, w_layer, *, seq):
    n_batch, t, d = xs.shape
    d_v = o_f.shape[2]
    tm = _row_tile(t) // 2
    assert tm % 16 == 0
    gate_blk = (p.shape[2] - d_v) // d_v
    kern = functools.partial(_outproj_hg_kernel, n_batch=n_batch, seq=seq, t_total=t)
    rows = lambda width, cblk: pl.BlockSpec((None, tm, width), lambda b, i: (b, i, cblk))
    return pl.pallas_call(
        kern,
        out_shape=jax.ShapeDtypeStruct((n_batch, t, d), F32),
        grid=(n_batch, t // tm),
        in_specs=[
            rows(d_v, 0), rows(d_v, 0), rows(d_v, gate_blk),
            pl.BlockSpec((1, norm_w.shape[0]), lambda b, i: (0, 0)),
            pl.BlockSpec((None, d_v, d), lambda b, i: (w_layer, 0, 0)),
            rows(d, 0),
            _mod_spec(d, 2, layer, 2),
        ],
        out_specs=rows(d, 0),
        scratch_shapes=[pltpu.VMEM((tm, d_v), BF16)],
        compiler_params=_cparams("arbitrary", "arbitrary"),
        name="out_proj_hgrn2",
    )(o_f, o_b, p, norm_w.reshape(1, -1), w_out, xs, mods)


def _rot_columns(w):
    q = QK_ROPE // 4
    return jnp.concatenate([-w[..., q:2 * q], w[..., :q], -w[..., 3 * q:], w[..., 2 * q:3 * q]], axis=-1)


def _rope_tables(seq, n_ctx):
    t = np.arange(seq)
    row, colm = t // GRID_W, t % GRID_W
    half = QK_ROPE // 2
    inv_freq = ROPE_BASE ** (-np.arange(0, half, 2, dtype=np.float32) / half)
    ang_r = row.astype(np.float32)[:, None] * inv_freq
    ang_c = colm.astype(np.float32)[:, None] * inv_freq
    ang = jnp.asarray(np.concatenate([ang_r, ang_r, ang_c, ang_c], axis=-1), F32)
    cos = jnp.concatenate([jnp.cos(ang), jnp.ones((n_ctx, QK_ROPE), F32)], axis=0)
    sin = jnp.concatenate([jnp.sin(ang), jnp.zeros((n_ctx, QK_ROPE), F32)], axis=0)
    return cos, sin


def kernel(x, c, ctx, c_ctx, ada_w, ada_b, norm_mix_w, norm_mlp_w, ab_w_in, ab_w_out, lru_conv_w, lru_conv_b, lru_w_a, lru_b_a, lru_w_x, lru_b_x, lru_lambda, mla_q_norm_w, mla_w_uq, mla_kv_norm_w, mla_w_ukv, hg_w_in, hg_lb_logits, hg_norm_w, hg_w_out, mlp_w1, mlp_w2, final_norm_w):
    n_batch, seq, d = x.shape
    n_ctx = ctx.shape[1]
    depth = ada_w.shape[0]
    assert n_batch + 1 <= SUBLANES

    xs = jnp.concatenate([x, ctx], axis=1)
    cond8 = jnp.zeros((SUBLANES, d), F32).at[:n_batch].set(c).at[n_batch].set(c_ctx)
    mods = _ada_all(cond8, ada_w, ada_b)
    lb_all = _lower_bounds(hg_lb_logits)
    cos_t, sin_t = _rope_tables(seq, n_ctx)

    d_lru = lru_conv_w.shape[-1]
    q_lora = mla_q_norm_w.shape[-1]
    col_kr = ab_w_in.shape[2] - QK_ROPE
    ab_w_in_b = ab_w_in.astype(BF16)
    w_kr = ab_w_in[..., col_kr:]
    w_kr_ext = jnp.concatenate([w_kr, _rot_columns(w_kr)], axis=-1).astype(BF16)
    tn_ab = 1024 if col_kr % 1024 == 0 else col_kr // 3
    assert tn_ab % LANES == 0 and col_kr % tn_ab == 0
    d_hg = hg_lb_logits.shape[-1]
    hg_w_in_b = hg_w_in.astype(BF16)
    tn_hg = 1024 if d_hg % 1024 == 0 else d_hg
    ab_w_out_b = ab_w_out.astype(BF16)
    hg_w_out_b = hg_w_out.astype(BF16)
    w1_b = mlp_w1.astype(BF16)
    w2_b = mlp_w2.astype(BF16)
    wq = mla_w_uq.reshape(-1, q_lora, MLA_HEADS, QK_NOPE + QK_ROPE)
    wq_ext = jnp.concatenate([wq, _rot_columns(wq[..., QK_NOPE:])], axis=-1)
    wq_ext = wq_ext.reshape(wq.shape[0], q_lora, -1).astype(BF16)
    wkv_b = mla_w_ukv.astype(BF16)
    lru_w_a_b = lru_w_a.astype(BF16)
    lru_w_x_b = lru_w_x.astype(BF16)

    for l in range(depth):
        if l % 2 == 0:
            e = l // 2
            p, p_kr = _in_proj(xs, norm_mix_w[l], mods, ab_w_in_b, l, e, seq=seq, tn=tn_ab,
                               n_cols=col_kr, w_extra=w_kr_ext)
            ya = _lru(p, lru_conv_w[e], lru_conv_b[e], lru_w_a_b[e], lru_b_a[e],
                      lru_w_x_b[e], lru_b_x[e], lru_lambda[e], seq=seq)
            q, k, vt = _mla_qkv(p, p_kr, mla_q_norm_w[e], mla_kv_norm_w[e], wq_ext[e], wkv_b[e],
                                cos_t, sin_t, seq=seq, col0=2 * d_lru)
            yb = _attention(q, k, vt, seq=seq)
            xs = _out_proj_ab(ya, yb, ab_w_out_b, xs, mods, l, e, seq=seq)
        else:
            o = l // 2
            p_qig, p_f = _in_proj(xs, norm_mix_w[l], mods, hg_w_in_b, l, o, seq=seq, tn=tn_hg,
                                  f32_cols=(d_hg, 3 * d_hg))
            o_f, o_b = _hgrn2(p_qig, p_f, lb_all[l], seq=seq)
            xs = _out_proj_hg(o_f, o_b, p_qig, hg_norm_w[o], hg_w_out_b, xs, mods, l, o, seq=seq)
        last = l == depth - 1
        xs = _mlp(xs, norm_mlp_w[l], mods, w1_b, w2_b, l, seq=seq,
                  final_norm_w=final_norm_w if last else None)
    return xs
```

```python
import functools

import numpy as np
import jax
import jax.numpy as jnp
from jax import lax
from jax.experimental import pallas as pl
from jax.experimental.pallas import tpu as pltpu

GRID_W = 64
EPS = 1e-6
N_MOD = 6
LRU_HEADS = 8
CONV_W = 4
CONV_LEFT = CONV_W // 2
LRU_C = 8.0
MLA_HEADS = 8
QK_NOPE = 128
QK_ROPE = 64
V_HEAD = 128
ROPE_BASE = 10000.0
HG_DK = 128
HG_CHUNK = 64

LANES = 128
SUBLANES = 8
VMEM_LIMIT_BYTES = 56 * 1024 * 1024
NORM_ROW_CHUNK = 32
ATTN_KEY_CHUNK = 1024
ATTN_Q_TILE = 512
ATTN_TILES_PER_STEP = 4
MLP_LATENT_ROW_TILE = 512

F32 = jnp.float32
BF16 = jnp.bfloat16

HG_LEVELS = int(np.log2(HG_CHUNK))


def _cparams(*sem):
    return pltpu.CompilerParams(dimension_semantics=sem, vmem_limit_bytes=VMEM_LIMIT_BYTES)


def _dot(a, b):
    return jnp.dot(a, b, preferred_element_type=F32)


def _dot_nt(a, b):
    return lax.dot_general(a, b, (((1,), (1,)), ((), ())), preferred_element_type=F32)


def _dot_tn(a, b):
    return lax.dot_general(a, b, (((0,), (0,)), ((), ())), preferred_element_type=F32)


LOG2E = np.float32(np.log2(np.e))


def _sigmoid(x):
    return 1.0 / (1.0 + jnp.exp2(x * (-LOG2E)))


def _silu(x):
    return x * _sigmoid(x)


def _gelu_tanh(x):
    c = np.float32(np.sqrt(2.0 / np.pi))
    return 0.5 * x * (1.0 + jnp.tanh(c * (x + np.float32(0.044715) * (x * x * x))))


def _softplus(x):
    return jnp.maximum(x, 0.0) + jnp.log1p(jnp.exp(-jnp.abs(x)))


def _row_tile(t_total):
    tm = t_total // 8
    assert tm * 8 == t_total and tm % 16 == 0, t_total
    return tm


def _ada_kernel(c_ref, w_ref, b_ref, o_ref):
    a = _silu(c_ref[...]).astype(BF16)
    o_ref[...] = _dot(a, w_ref[...].astype(BF16)) + b_ref[...]


def _ada_all(cond8, ada_w, ada_b):
    n_layers, d, n6 = ada_w.shape
    tn = 1024 if n6 % 1024 == 0 else n6 // N_MOD
    return pl.pallas_call(
        _ada_kernel,
        out_shape=jax.ShapeDtypeStruct((n_layers, SUBLANES, n6), F32),
        grid=(n_layers, n6 // tn),
        in_specs=[
            pl.BlockSpec((SUBLANES, d), lambda l, j: (0, 0)),
            pl.BlockSpec((None, d, tn), lambda l, j: (l, 0, j)),
            pl.BlockSpec((None, 1, tn), lambda l, j: (l, 0, j)),
        ],
        out_specs=pl.BlockSpec((None, SUBLANES, tn), lambda l, j: (l, 0, j)),
        compiler_params=_cparams("arbitrary", "arbitrary"),
        name="ada_params",
    )(cond8, ada_w, ada_b.reshape(n_layers, 1, n6))


def _lb_kernel(x_ref, o_ref):
    n = x_ref.shape[0]
    rows = [x_ref[pl.ds(l, 1), :] for l in range(n)]
    mx = functools.reduce(jnp.maximum, rows)
    es = [jnp.exp(r - mx) for r in rows]
    tot = functools.reduce(lambda a, b: a + b, es)
    ps = [e / tot for e in es]
    acc = ps[0]
    o_ref[pl.ds(0, 1), :] = acc - ps[0]
    for l in range(1, n):
        acc = acc + ps[l]
        o_ref[pl.ds(l, 1), :] = acc - ps[0]


def _lower_bounds(logits):
    return pl.pallas_call(
        _lb_kernel, out_shape=jax.ShapeDtypeStruct(logits.shape, F32), name="hgrn2_lower_bounds"
    )(logits.astype(F32))


def _for_row_groups(tm, t_total, seq, n_batch, fn):
    b = pl.program_id(0)
    i = pl.program_id(1)
    i_mixed, split = divmod(seq, tm)
    if i_mixed > 0:
        pl.when(i < i_mixed)(lambda: fn(0, tm, b))
    if i_mixed < t_total // tm:
        @pl.when(i == i_mixed)
        def _():
            if split > 0:
                fn(0, split, b)
            fn(split, tm, n_batch)
    if i_mixed + 1 < t_total // tm:
        pl.when(i > i_mixed)(lambda: fn(0, tm, n_batch))


def _norm_modulate_rows(x_ref, nw_ref, sh_ref, sc_ref, h_ref, r0, r1, rid):
    gain = nw_ref[...] * (1.0 + sc_ref[pl.ds(rid, 1), :])
    shift = sh_ref[pl.ds(rid, 1), :]
    rc = NORM_ROW_CHUNK
    for r in range(r0, r1, rc):
        x = x_ref[r:min(r + rc, r1), :]
        ms = jnp.mean(x * x, axis=-1, keepdims=True)
        h_ref[r:min(r + rc, r1), :] = (x * lax.rsqrt(ms + EPS) * gain + shift).astype(h_ref.dtype)


def _mod_spec(d, k, layer, nargs):
    if nargs == 2:
        return pl.BlockSpec((None, SUBLANES, d), lambda b, i: (layer, 0, k))
    return pl.BlockSpec((None, SUBLANES, d), lambda b, i, j: (layer, 0, k))


def _inproj_kernel(*refs, n_batch, seq, t_total, f32_tiles, has_extra):
    x_ref, nw_ref, sh_ref, sc_ref, w_ref = refs[:5]
    h_ref = refs[-1]
    rest = list(refs[5:-1])
    wx_ref = rest.pop(0) if has_extra else None
    lo_ref = rest.pop(0)
    hi_ref = rest.pop(0) if f32_tiles else None
    ox_ref = rest.pop(0) if has_extra else None
    tm = x_ref.shape[0]
    j = pl.program_id(2)

    @pl.when(j == 0)
    def _():
        _for_row_groups(tm, t_total, seq, n_batch, functools.partial(
            _norm_modulate_rows, x_ref, nw_ref, sh_ref, sc_ref, h_ref))
        if has_extra:
            ox_ref[...] = _dot(h_ref[...], wx_ref[...]).astype(ox_ref.dtype)

    y = _dot(h_ref[...], w_ref[...])
    if not f32_tiles:
        lo_ref[...] = y.astype(lo_ref.dtype)
    else:
        is_f32 = jnp.logical_and(j >= f32_tiles[0], j < f32_tiles[1])

        @pl.when(jnp.logical_not(is_f32))
        def _():
            lo_ref[...] = y.astype(lo_ref.dtype)

        @pl.when(is_f32)
        def _():
            hi_ref[...] = y.astype(hi_ref.dtype)


def _in_proj(xs, norm_w, mods, w, layer, w_layer, *, seq, tn, n_cols=None, f32_cols=None, w_extra=None):
    n_batch, t, d = xs.shape
    n_out = w.shape[2] if n_cols is None else n_cols
    tm = _row_tile(t) * 2
    assert n_out % tn == 0
    n_tiles = n_out // tn
    j0, j1 = (0, 0) if f32_cols is None else (f32_cols[0] // tn, f32_cols[1] // tn)
    assert f32_cols is None or (f32_cols[0] % tn == 0 and f32_cols[1] % tn == 0 and j0 < j1 <= n_tiles)
    n_hi = j1 - j0
    kern = functools.partial(_inproj_kernel, n_batch=n_batch, seq=seq, t_total=t,
                             f32_tiles=(j0, j1) if n_hi else None, has_extra=w_extra is not None)

    def lo_idx(j):
        return jnp.where(j < j0, j, jnp.where(j < j1, max(j0 - 1, 0), j - n_hi))

    in_specs = [
        pl.BlockSpec((None, tm, d), lambda b, i, j: (b, i, 0)),
        pl.BlockSpec((1, d), lambda b, i, j: (0, 0)),
        _mod_spec(d, 0, layer, 3),
        _mod_spec(d, 1, layer, 3),
        pl.BlockSpec((None, d, tn), lambda b, i, j: (w_layer, 0, j)),
    ]
    args = [xs, norm_w.reshape(1, d), mods, mods, w]
    out_shape = [jax.ShapeDtypeStruct((n_batch, t, (n_tiles - n_hi) * tn), BF16)]
    out_specs = [pl.BlockSpec((None, tm, tn), lambda b, i, j: (b, i, lo_idx(j)))]
    if n_hi:
        out_shape.append(jax.ShapeDtypeStruct((n_batch, t, n_hi * tn), F32))
        out_specs.append(pl.BlockSpec((None, tm, tn), lambda b, i, j: (b, i, jnp.clip(j - j0, 0, n_hi - 1))))
    if w_extra is not None:
        in_specs.append(pl.BlockSpec((None, d, LANES), lambda b, i, j: (w_layer, 0, 0)))
        args.append(w_extra)
        out_shape.append(jax.ShapeDtypeStruct((n_batch, t, LANES), BF16))
        out_specs.append(pl.BlockSpec((None, tm, LANES), lambda b, i, j: (b, i, 0)))
    outs = pl.pallas_call(
        kern,
        out_shape=out_shape,
        grid=(n_batch, t // tm, n_tiles),
        in_specs=in_specs,
        out_specs=out_specs,
        scratch_shapes=[pltpu.VMEM((tm, d), BF16)],
        compiler_params=_cparams("arbitrary", "arbitrary", "arbitrary"),
        name="in_proj",
    )(*args)
    return outs if len(outs) > 1 else outs[0]


def _mlp_kernel(x_ref, nw_ref, sh_ref, sc_ref, g_ref, w1_ref, w2_ref, *rest, n_batch, seq, t_total):
    fw_ref = rest[0] if len(rest) == 4 else None
    o_ref, h_ref, acc_ref = rest[-3:]
    f = pl.program_id(2)
    tm = x_ref.shape[0]

    @pl.when(f == 0)
    def _():
        _for_row_groups(tm, t_total, seq, n_batch, functools.partial(
            _norm_modulate_rows, x_ref, nw_ref, sh_ref, sc_ref, h_ref))
        acc_ref[...] = jnp.zeros_like(acc_ref)

    a = jnp.maximum(_dot(h_ref[...], w1_ref[...]), 0.0)
    acc_ref[...] += _dot((a * a).astype(BF16), w2_ref[...])

    @pl.when(f == pl.num_programs(2) - 1)
    def _():
        def finish(r0, r1, rid):
            rc = NORM_ROW_CHUNK if fw_ref is not None else r1 - r0
            for r in range(r0, r1, rc):
                rs = slice(r, min(r + rc, r1))
                y = x_ref[rs, :] + g_ref[pl.ds(rid, 1), :] * acc_ref[rs, :]
                if fw_ref is not None:
                    y = y * lax.rsqrt(jnp.mean(y * y, axis=-1, keepdims=True) + EPS) * fw_ref[...]
                o_ref[rs, :] = y

        _for_row_groups(tm, t_total, seq, n_batch, finish)


def _mlp(xs, norm_w, mods, w1, w2, layer, *, seq, final_norm_w=None):
    n_batch, t, d = xs.shape
    d_ff = w1.shape[2]
    if final_norm_w is None:
        rows, tm = t, _row_tile(t)
    else:
        rows = seq
        tm = MLP_LATENT_ROW_TILE if seq % MLP_LATENT_ROW_TILE == 0 else t - seq
        assert seq % tm == 0
    tf = 1024 if d_ff % 1024 == 0 else d_ff
    kern = functools.partial(_mlp_kernel, n_batch=n_batch, seq=seq, t_total=rows)
    in_specs = [
        pl.BlockSpec((None, tm, d), lambda b, i, f: (b, i, 0)),
        pl.BlockSpec((1, d), lambda b, i, f: (0, 0)),
        _mod_spec(d, 3, layer, 3),
        _mod_spec(d, 4, layer, 3),
        _mod_spec(d, 5, layer, 3),
        pl.BlockSpec((None, d, tf), lambda b, i, f: (layer, 0, f)),
        pl.BlockSpec((None, tf, d), lambda b, i, f: (layer, f, 0)),
    ]
    args = [xs, norm_w.reshape(1, d), mods, mods, mods, w1, w2]
    if final_norm_w is not None:
        in_specs.append(pl.BlockSpec((1, d), lambda b, i, f: (0, 0)))
        args.append(final_norm_w.reshape(1, d))
    return pl.pallas_call(
        kern,
        out_shape=jax.ShapeDtypeStruct((n_batch, rows, d), F32),
        grid=(n_batch, rows // tm, d_ff // tf),
        in_specs=in_specs,
        out_specs=pl.BlockSpec((None, tm, d), lambda b, i, f: (b, i, 0)),
        scratch_shapes=[pltpu.VMEM((tm, d), BF16), pltpu.VMEM((tm, d), F32)],
        compiler_params=_cparams("arbitrary", "arbitrary", "arbitrary"),
        name="mlp",
    )(*args)


def _outproj_ab_kernel(ya_ref, yb_ref, w_ref, x_ref, g_ref, o_ref, *, n_batch, seq, t_total):
    tm = x_ref.shape[0]
    da = ya_ref.shape[1]
    y = _dot(ya_ref[...], w_ref[0:da, :]) + _dot(yb_ref[...], w_ref[da:, :])

    def finish(r0, r1, rid):
        o_ref[r0:r1, :] = x_ref[r0:r1, :] + g_ref[pl.ds(rid, 1), :] * y[r0:r1, :]

    _for_row_groups(tm, t_total, seq, n_batch, finish)


def _out_proj_ab(ya, yb, w_out, xs, mods, layer, w_layer, *, seq):
    n_batch, t, d = xs.shape
    da, db = ya.shape[2], yb.shape[2]
    tm = _row_tile(t)
    kern = functools.partial(_outproj_ab_kernel, n_batch=n_batch, seq=seq, t_total=t)
    return pl.pallas_call(
        kern,
        out_shape=jax.ShapeDtypeStruct((n_batch, t, d), F32),
        grid=(n_batch, t // tm),
        in_specs=[
            pl.BlockSpec((None, tm, da), lambda b, i: (b, i, 0)),
            pl.BlockSpec((None, tm, db), lambda b, i: (b, i, 0)),
            pl.BlockSpec((None, da + db, d), lambda b, i: (w_layer, 0, 0)),
            pl.BlockSpec((None, tm, d), lambda b, i: (b, i, 0)),
            _mod_spec(d, 2, layer, 2),
        ],
        out_specs=pl.BlockSpec((None, tm, d), lambda b, i: (b, i, 0)),
        compiler_params=_cparams("arbitrary", "arbitrary"),
        name="out_proj_ab",
    )(ya, yb, w_out, xs, mods)


def _lru_kernel(gate_ref, u_ref, cw_ref, cb_ref, wa_ref, ba_ref, wx_ref, bx_ref, lam_ref, o_ref,
                upad_ref, af_ref, bf_ref, ab_ref, bb_ref, hf_ref, hb_ref, *, seq, chunk):
    t_total = u_ref.shape[0]
    n_ctx = t_total - seq
    pad = SUBLANES
    cw = cw_ref[...]
    cb = cb_ref[...]
    log2_a_unit = _softplus(-lam_ref[...]) * (-LRU_C * LOG2E)
    zeros_pad = jnp.zeros((pad, LANES), F32)

    def gates(uc, d):
        ucb = uc.astype(BF16)
        r = _sigmoid(_dot(ucb, wa_ref[d]) + ba_ref[pl.ds(d, 1), :])
        i = _sigmoid(_dot(ucb, wx_ref[d]) + bx_ref[pl.ds(d, 1), :])
        a = jnp.exp2(r * log2_a_unit[d:d + 1, :])
        return a, jnp.sqrt(1.0 - a * a) * (i * uc)

    def prepare(src0, dst0, n):
        upad_ref[pl.ds(0, pad), :] = zeros_pad
        upad_ref[pl.ds(pad + n, pad), :] = zeros_pad
        upad_ref[pl.ds(pad, n), :] = u_ref[pl.ds(src0, n), :].astype(F32)
        ch = min(chunk, n)

        def body(c, carry):
            r0 = pl.multiple_of(c * ch, SUBLANES)
            win = upad_ref[pl.ds(r0, ch + 2 * pad), :]
            uc = cb
            for j in range(CONV_W):
                off = pad + j - CONV_LEFT
                uc = uc + win[off:off + ch, :] * cw[j:j + 1, :]
            a0, b0 = gates(uc, 0)
            a1, b1 = gates(uc, 1)
            d0 = pl.multiple_of(dst0 + r0, SUBLANES)
            af_ref[pl.ds(d0, ch), :] = a0
            bf_ref[pl.ds(d0, ch), :] = b0
            ab_ref[pl.ds(d0, ch), :] = a1
            bb_ref[pl.ds(d0, ch), :] = b1
            return carry

        lax.fori_loop(0, n // ch, body, 0)

    prepare(seq, 0, n_ctx)
    prepare(0, n_ctx, seq)

    row = lax.broadcasted_iota(jnp.int32, (SUBLANES, LANES), 0)

    def scan_fwd(a, b):
        for s in (1, 2, 4):
            m = row >= s
            a_s = pltpu.roll(a, s, 0)
            b_s = pltpu.roll(b, s, 0)
            b = jnp.where(m, a * b_s + b, b)
            a = jnp.where(m, a * a_s, a)
        return a, b

    def scan_bwd(a, b):
        for s in (1, 2, 4):
            m = row < SUBLANES - s
            a_s = pltpu.roll(a, SUBLANES - s, 0)
            b_s = pltpu.roll(b, SUBLANES - s, 0)
            b = jnp.where(m, a * b_s + b, b)
            a = jnp.where(m, a * a_s, a)
        return a, b

    def scan_range(g0, n_groups, carry):
        def body(g, c):
            cf, cbk = c
            rf = pl.multiple_of((g0 + g) * SUBLANES, SUBLANES)
            a, b = scan_fwd(af_ref[pl.ds(rf, SUBLANES), :], bf_ref[pl.ds(rf, SUBLANES), :])
            hf = a * cf + b
            hf_ref[pl.ds(rf, SUBLANES), :] = hf
            rb = pl.multiple_of((g0 + n_groups - 1 - g) * SUBLANES, SUBLANES)
            a, b = scan_bwd(ab_ref[pl.ds(rb, SUBLANES), :], bb_ref[pl.ds(rb, SUBLANES), :])
            hb = a * cbk + b
            hb_ref[pl.ds(rb, SUBLANES), :] = hb
            cf = jnp.broadcast_to(hf[SUBLANES - 1:SUBLANES, :], (SUBLANES, LANES))
            cbk = jnp.broadcast_to(hb[0:1, :], (SUBLANES, LANES))
            return cf, cbk

        return lax.fori_loop(0, n_groups, body, carry, unroll=4)

    zero = jnp.zeros((SUBLANES, LANES), F32)
    carry = scan_range(0, n_ctx // SUBLANES, (zero, zero))
    scan_range(n_ctx // SUBLANES, seq // SUBLANES, carry)

    def finish(src0, dst0, n):
        ch = min(chunk, n)

        def body(c, carry):
            r0 = c * ch
            s0 = pl.multiple_of(src0 + r0, SUBLANES)
            d0 = pl.multiple_of(dst0 + r0, SUBLANES)
            h = hf_ref[pl.ds(s0, ch), :] + hb_ref[pl.ds(s0, ch), :]
            gate = gate_ref[pl.ds(d0, ch), :].astype(F32)
            o_ref[pl.ds(d0, ch), :] = (h * _gelu_tanh(gate)).astype(o_ref.dtype)
            return carry

        lax.fori_loop(0, n // ch, body, 0)

    finish(0, seq, n_ctx)
    finish(n_ctx, 0, seq)


def _lru(p, conv_w, conv_b, w_a, b_a, w_x, b_x, lam, *, seq):
    n_batch, t, _ = p.shape
    d_lru = conv_w.shape[1]
    n_heads = w_a.shape[1]
    assert d_lru // n_heads == LANES
    n_ctx = t - seq
    chunk = 256
    assert seq % min(chunk, seq) == 0 and n_ctx % min(chunk, n_ctx) == 0
    kern = functools.partial(_lru_kernel, seq=seq, chunk=chunk)
    vec = lambda rows: pl.BlockSpec((rows, LANES), lambda b, h: (0, h))
    wspec = pl.BlockSpec((2, None, LANES, LANES), lambda b, h: (0, h, 0, 0))
    scratch_rows = pltpu.VMEM((t, LANES), F32)
    return pl.pallas_call(
        kern,
        out_shape=jax.ShapeDtypeStruct((n_batch, t, d_lru), BF16),
        grid=(n_batch, n_heads),
        in_specs=[
            pl.BlockSpec((None, t, LANES), lambda b, h: (b, 0, h)),
            pl.BlockSpec((None, t, LANES), lambda b, h: (b, 0, n_heads + h)),
            vec(CONV_W), vec(1), wspec, vec(2), wspec, vec(2), vec(2),
        ],
        out_specs=pl.BlockSpec((None, t, LANES), lambda b, h: (b, 0, h)),
        scratch_shapes=[pltpu.VMEM((max(seq, n_ctx) + 2 * SUBLANES, LANES), F32),
                        scratch_rows, scratch_rows, scratch_rows, scratch_rows,
                        scratch_rows, scratch_rows],
        compiler_params=_cparams("arbitrary", "arbitrary"),
        name="rglru",
    )(p, p, conv_w, conv_b.reshape(1, d_lru), w_a, b_a, w_x, b_x, lam)


def _mla_qkv_kernel(cq_ref, ckv_ref, kr_ref, qn_ref, kvn_ref, wq_ref, wkv_ref, cos_ref, sin_ref,
                    q_ref, k_ref, v_ref, *, scale):
    def rms(x, w):
        return (x * lax.rsqrt(jnp.mean(x * x, axis=-1, keepdims=True) + EPS) * w).astype(BF16)

    cos = cos_ref[...]
    sin = sin_ref[...]
    q_all = _dot(rms(cq_ref[...].astype(F32), qn_ref[...]), wq_ref[...])
    kv_all = _dot(rms(ckv_ref[...].astype(F32), kvn_ref[...]), wkv_ref[...])
    kr = kr_ref[...].astype(F32)
    kr_rot = kr[:, :QK_ROPE] * cos + kr[:, QK_ROPE:] * sin
    wq_head = QK_NOPE + 2 * QK_ROPE
    wkv_head = QK_NOPE + V_HEAD
    for h in range(q_ref.shape[0]):
        qh = q_all[:, h * wq_head:(h + 1) * wq_head]
        qr = qh[:, QK_NOPE:QK_NOPE + QK_ROPE] * cos + qh[:, QK_NOPE + QK_ROPE:] * sin
        q_ref[h, :, 0:QK_NOPE] = (qh[:, :QK_NOPE] * scale).astype(BF16)
        q_ref[h, :, QK_NOPE:] = (qr * scale).astype(BF16)
        kvh = kv_all[:, h * wkv_head:(h + 1) * wkv_head]
        k_ref[h, :, 0:QK_NOPE] = kvh[:, :QK_NOPE].astype(BF16)
        k_ref[h, :, QK_NOPE:] = kr_rot.astype(BF16)
        v_ref[h] = kvh[:, QK_NOPE:].T.astype(BF16)


def _mla_qkv(p, p_kr, q_norm_w, kv_norm_w, wq_ext, wkv, cos_t, sin_t, *, seq, col0):
    n_batch, t, _ = p.shape
    q_lora = q_norm_w.shape[0]
    kv_lora = kv_norm_w.shape[0]
    assert q_lora == kv_lora and col0 % q_lora == 0 and p_kr.shape[2] == 2 * QK_ROPE == LANES
    n_ctx = t - seq
    tm = n_ctx
    nh = MLA_HEADS
    dqk = QK_NOPE + QK_ROPE
    scale = np.float32(dqk ** -0.5 * np.log2(np.e))
    cq_blk = col0 // q_lora
    full = lambda shape: pl.BlockSpec(shape, lambda b, i: (0,) * len(shape))
    return pl.pallas_call(
        functools.partial(_mla_qkv_kernel, scale=scale),
        out_shape=(jax.ShapeDtypeStruct((n_batch, nh, t, dqk), BF16),
                   jax.ShapeDtypeStruct((n_batch, nh, t, dqk), BF16),
                   jax.ShapeDtypeStruct((n_batch, nh, V_HEAD, t), BF16)),
        grid=(n_batch, t // tm),
        in_specs=[
            pl.BlockSpec((None, tm, q_lora), lambda b, i: (b, i, cq_blk)),
            pl.BlockSpec((None, tm, kv_lora), lambda b, i: (b, i, cq_blk + 1)),
            pl.BlockSpec((None, tm, LANES), lambda b, i: (b, i, 0)),
            full((1, q_lora)), full((1, kv_lora)), full(wq_ext.shape), full(wkv.shape),
            pl.BlockSpec((tm, QK_ROPE), lambda b, i: (i, 0)),
            pl.BlockSpec((tm, QK_ROPE), lambda b, i: (i, 0)),
        ],
        out_specs=(pl.BlockSpec((None, nh, tm, dqk), lambda b, i: (b, 0, i, 0)),
                   pl.BlockSpec((None, nh, tm, dqk), lambda b, i: (b, 0, i, 0)),
                   pl.BlockSpec((None, nh, V_HEAD, tm), lambda b, i: (b, 0, 0, i))),
        compiler_params=_cparams("arbitrary", "arbitrary"),
        name="mla_qkv",
    )(p, p, p_kr, q_norm_w.reshape(1, q_lora), kv_norm_w.reshape(1, kv_lora), wq_ext, wkv, cos_t, sin_t)


def _softmax_values(qs, k_ref, vt_ref, s_ref):
    t = k_ref.shape[0]
    bounds = list(range(0, t, ATTN_KEY_CHUNK)) + [t]
    chunks = list(zip(bounds[:-1], bounds[1:]))
    n = len(qs)
    ms, ls, outs = [None] * n, [None] * n, [None] * n
    for stage in range(n + 1):
        for a, b in chunks:
            if stage < n:
                s = _dot_nt(k_ref[a:b, :], qs[stage])
                s_ref[stage, a:b, :] = s
                cm = jnp.max(s, axis=0, keepdims=True)
                ms[stage] = cm if ms[stage] is None else jnp.maximum(ms[stage], cm)
            if stage >= 1:
                i = stage - 1
                p = jnp.exp2(s_ref[i, a:b, :] - ms[i])
                ps = jnp.sum(p, axis=0, keepdims=True)
                pv = _dot(vt_ref[:, a:b], p.astype(BF16))
                ls[i] = ps if ls[i] is None else ls[i] + ps
                outs[i] = pv if outs[i] is None else outs[i] + pv
    return [(outs[i] * (1.0 / ls[i])).T for i in range(n)]


def _attn_kernel(q_ref, k_ref, vt_ref, o_ref, s_ref):
    n = s_ref.shape[0]
    tq = q_ref.shape[0] // n
    outs = _softmax_values([q_ref[i * tq:(i + 1) * tq, :] for i in range(n)], k_ref, vt_ref, s_ref)
    for i in range(n):
        o_ref[i * tq:(i + 1) * tq, :] = outs[i].astype(o_ref.dtype)


def _attention(q, k, vt, *, seq):
    n_batch, nh, t, dqk = q.shape
    n_ctx = t - seq
    tq = ATTN_Q_TILE if seq % ATTN_Q_TILE == 0 else n_ctx
    assert seq % tq == 0 and seq % n_ctx == 0
    nq = ATTN_TILES_PER_STEP if seq % (ATTN_TILES_PER_STEP * tq) == 0 else 1
    y_lat = pl.pallas_call(
        _attn_kernel,
        out_shape=jax.ShapeDtypeStruct((n_batch, seq, nh * V_HEAD), BF16),
        grid=(n_batch, nh, seq // (nq * tq)),
        in_specs=[
            pl.BlockSpec((None, None, nq * tq, dqk), lambda b, h, i: (b, h, i, 0)),
            pl.BlockSpec((None, None, t, dqk), lambda b, h, i: (b, h, 0, 0)),
            pl.BlockSpec((None, None, V_HEAD, t), lambda b, h, i: (b, h, 0, 0)),
        ],
        out_specs=pl.BlockSpec((None, nq * tq, V_HEAD), lambda b, h, i: (b, i, h)),
        scratch_shapes=[pltpu.VMEM((nq, t, tq), F32)],
        compiler_params=_cparams("arbitrary", "arbitrary", "arbitrary"),
        name="mla_attention",
    )(q, k, vt)
    ctx_blk = seq // n_ctx
    y_ctx = pl.pallas_call(
        _attn_kernel,
        out_shape=jax.ShapeDtypeStruct((n_batch, n_ctx, nh * V_HEAD), BF16),
        grid=(n_batch, nh),
        in_specs=[
            pl.BlockSpec((None, None, n_ctx, dqk), lambda b, h: (b, h, ctx_blk, 0)),
            pl.BlockSpec((None, None, n_ctx, dqk), lambda b, h: (b, h, ctx_blk, 0)),
            pl.BlockSpec((None, None, V_HEAD, n_ctx), lambda b, h: (b, h, 0, ctx_blk)),
        ],
        out_specs=pl.BlockSpec((None, n_ctx, V_HEAD), lambda b, h: (b, 0, h)),
        scratch_shapes=[pltpu.VMEM((1, n_ctx, n_ctx), F32)],
        compiler_params=_cparams("arbitrary", "arbitrary"),
        name="mla_attention_ctx",
    )(q, k, vt)
    return jnp.concatenate([y_lat, y_ctx], axis=1)


def _hgrn2_tables(tt):
    n = HG_CHUNK
    t = np.arange(n)[:, None]
    r = np.arange(n)[None, :]
    fwd = [(r <= t), (r > t)]
    bwd = [(r >= t), (r < t)]
    for lvl in range(HG_LEVELS):
        m = 1 << lvl
        e = (t >> (lvl + 1)) * 2 * m + m - 1
        right = ((t >> lvl) & 1) == 1
        fwd.append(np.where(right, (r > e) & (r <= t), (r > t) & (r <= e)))
        bwd.append(np.where(right, (r > e) & (r < t), (r >= t) & (r <= e)))
    stack = lambda ms: np.tile(np.concatenate(ms, axis=0).astype(np.float32), (1, 2))
    tq = np.arange(tt)[:, None]
    ts = np.arange(tt)[None, :]
    level = np.floor(np.log2(np.maximum(tq ^ ts, 1))).astype(np.int32)
    lv_f = np.where(ts < tq, level, -1).astype(np.int32)
    return stack(fwd), stack(bwd), lv_f, np.ascontiguousarray(lv_f.T)


def _hgrn2_kernel(q_ref, f_ref, v_ref, lb_ref, m_ref, lv_ref, o_ref, s_ref, *, heads):
    n = HG_CHUNK
    tt = q_ref.shape[0]
    wb = q_ref.shape[1]
    nc = tt // n
    n_levels = HG_LEVELS + int(np.log2(nc))

    @pl.when(pl.program_id(3) == 0)
    def _():
        s_ref[...] = jnp.zeros_like(s_ref)

    lb = lb_ref[...]
    row = lax.broadcasted_iota(jnp.int32, (tt, 1), 0)

    def prod(vs):
        return functools.reduce(lambda a, b: a * b, vs) if vs else None

    def by_chunk(pieces):
        return jnp.concatenate(
            [x[c * n:(c + 1) * n] if m is None else x[c * n:(c + 1) * n] * m
             for c, (x, m) in enumerate(pieces)], axis=0)

    def one_direction(q_ref, f_ref, v_ref, m_ref, lv_ref, o_ref, s_ref, backward):
        f = lb + (1.0 - lb) * _sigmoid(f_ref[...])
        g = jnp.log(f) * LOG2E
        g_hi = g.astype(BF16)
        g_lo = (g - g_hi.astype(F32)).astype(BF16)
        lanes = lambda x: jnp.concatenate([x[c * n:(c + 1) * n] for c in range(nc)], axis=1)
        g2 = jnp.concatenate([lanes(g_hi), lanes(g_lo)], axis=0)
        x_all = jnp.exp2(_dot(m_ref[...], g2))
        q_all = _silu(q_ref[...].astype(F32))
        k_all = 1.0 - f
        v_all = v_ref[...].astype(F32)
        lv = lv_ref[...]
        outs = []
        for h in range(heads):
            cs = slice(h * HG_DK, (h + 1) * HG_DK)
            q, k, v = q_all[:, cs], k_all[:, cs], v_all[:, cs]
            vb = v.astype(BF16)

            def xblk(r):
                return jnp.concatenate(
                    [x_all[r * n:(r + 1) * n, c * wb + h * HG_DK:c * wb + (h + 1) * HG_DK]
                     for c in range(nc)], axis=0)

            eq_c = xblk(0)
            qe = q * eq_c
            ke = k * xblk(1)
            edge = (lambda c: c * n) if backward else (lambda c: c * n + n - 1)
            tot = [eq_c[edge(c):edge(c) + 1] for c in range(nc)]
            before = lambda c, lo: prod([tot[j] for j in range(lo, c)])
            after = lambda c, hi: prod([tot[j] for j in range(c + 1, hi)])

            a = jnp.zeros((tt, tt), BF16)
            for lvl in range(n_levels):
                if lvl < HG_LEVELS:
                    is_q = ((row >> lvl) & 1) == (0 if backward else 1)
                    z = xblk(2 + lvl) * jnp.where(is_q, q, k)
                else:
                    mc = 1 << (lvl - HG_LEVELS)
                    pieces = []
                    for c in range(nc):
                        lo = (c // mc) * mc
                        is_left = ((c // mc) & 1) == 0
                        if is_left == backward:
                            pieces.append((qe, after(c, lo + mc) if backward else before(c, lo)))
                        else:
                            pieces.append((ke, before(c, lo) if backward else after(c, lo + mc)))
                    z = by_chunk(pieces)
                zb = z.astype(BF16)
                a = jnp.where(lv == lvl, _dot_nt(zb, zb).astype(BF16), a)

            if backward:
                qe_t = by_chunk([(qe, after(c, nc)) for c in range(nc)])
                ke_t = by_chunk([(ke, before(c, 0)) for c in range(nc)])
            else:
                qe_t = by_chunk([(qe, before(c, 0)) for c in range(nc)])
                ke_t = by_chunk([(ke, after(c, nc)) for c in range(nc)])
            s_t = s_ref[h]
            o = _dot_nt(qe_t.astype(BF16), s_t.astype(BF16)) + _dot(a, vb)
            o = o + jnp.sum(q * k, axis=-1, keepdims=True) * v
            s_ref[h] = s_t * prod(tot) + _dot_tn(vb, ke_t.astype(BF16))
            outs.append(o)
        o_all = outs[0] if heads == 1 else jnp.concatenate(outs, axis=1)
        o_ref[...] = o_all.astype(o_ref.dtype)

    is_bwd = pl.program_id(2) == 1
    pl.when(jnp.logical_not(is_bwd))(
        lambda: one_direction(q_ref, f_ref, v_ref, m_ref, lv_ref, o_ref, s_ref, False))
    pl.when(is_bwd)(lambda: one_direction(q_ref, f_ref, v_ref, m_ref, lv_ref, o_ref, s_ref, True))


def _hgrn2(p_qig, p_f, lb, *, seq, heads_per_step=8):
    n_batch, t, _ = p_qig.shape
    d_k = lb.shape[-1]
    n_heads = d_k // HG_DK
    hb = heads_per_step if n_heads % heads_per_step == 0 else 1
    wb = hb * HG_DK
    tt = t - seq
    assert seq % tt == 0 and tt % HG_CHUNK == 0
    assert p_qig.shape[2] == 3 * d_k and p_f.shape[2] == 2 * d_k
    assert (tt // HG_CHUNK) & (tt // HG_CHUNK - 1) == 0
    nt = t // tt
    ncol = d_k // wb
    mf, mb, lvf, lvb = _hgrn2_tables(tt)

    def tile(d, s):
        return jnp.where(s == 0, nt - 1, jnp.where(d == 0, s - 1, nt - 1 - s))

    def col(grp):
        return pl.BlockSpec((None, tt, wb), lambda b, g, d, s: (b, tile(d, s), grp * ncol + g))

    table = lambda arr: pl.BlockSpec((None,) + arr.shape[1:], lambda b, g, d, s: (d, 0, 0))
    m_tab = jnp.asarray(np.stack([mf, mb]), BF16)
    lv_tab = jnp.asarray(np.stack([lvf, lvb]), BF16)
    return pl.pallas_call(
        functools.partial(_hgrn2_kernel, heads=hb),
        out_shape=jax.ShapeDtypeStruct((2, n_batch, t, d_k), BF16),
        grid=(n_batch, ncol, 2, nt),
        in_specs=[col(0),
                  pl.BlockSpec((None, tt, wb), lambda b, g, d, s: (b, tile(d, s), d * ncol + g)),
                  col(1),
                  pl.BlockSpec((1, wb), lambda b, g, d, s: (0, g)),
                  table(m_tab), table(lv_tab)],
        out_specs=pl.BlockSpec((None, None, tt, wb), lambda b, g, d, s: (d, b, tile(d, s), g)),
        scratch_shapes=[pltpu.VMEM((hb, HG_DK, HG_DK), F32)],
        compiler_params=_cparams("arbitrary", "arbitrary", "arbitrary", "arbitrary"),
        name="hgrn2_scan",
    )(p_qig, p_f, p_qig, lb.reshape(1, d_k), m_tab, lv_tab)


def _outproj_hg_kernel(of_ref, ob_ref, gate_ref, nw_ref, w_ref, x_ref, g_ref, o_ref, y_ref,
                       *, n_batch, seq, t_total):
    tm = x_ref.shape[0]
    nw = nw_ref[...]
    dv = nw.shape[1]
    for h in range(of_ref.shape[1] // dv):
        cs = slice(h * dv, (h + 1) * dv)
        o = of_ref[:, cs].astype(F32) + ob_ref[:, cs].astype(F32)
        y = o * lax.rsqrt(jnp.mean(o * o, axis=-1, keepdims=True) + EPS) * nw
        y_ref[:, cs] = (y * _silu(gate_ref[:, cs].astype(F32))).astype(BF16)
    y = _dot(y_ref[...], w_ref[...])

    def finish(r0, r1, rid):
        o_ref[r0:r1, :] = x_ref[r0:r1, :] + g_ref[pl.ds(rid, 1), :] * y[r0:r1, :]

    _for_row_groups(tm, t_total, seq, n_batch, finish)


def _out_proj_hg(o_dirs, p, norm_w, w_out, xs, mods, layer, w_layer, *, seq):
    n_batch, t, d = xs.shape
    d_v = o_dirs.shape[3]
    tm = _row_tile(t)
    gate_blk = (p.shape[2] - d_v) // d_v
    kern = functools.partial(_outproj_hg_kernel, n_batch=n_batch, seq=seq, t_total=t)
    rows = lambda width, cblk: pl.BlockSpec((None, tm, width), lambda b, i: (b, i, cblk))
    return pl.pallas_call(
        kern,
        out_shape=jax.ShapeDtypeStruct((n_batch, t, d), F32),
        grid=(n_batch, t // tm),
        in_specs=[
            pl.BlockSpec((None, None, tm, d_v), lambda b, i: (0, b, i, 0)),
            pl.BlockSpec((None, None, tm, d_v), lambda b, i: (1, b, i, 0)),
            rows(d_v, gate_blk),
            pl.BlockSpec((1, norm_w.shape[0]), lambda b, i: (0, 0)),
            pl.BlockSpec((None, d_v, d), lambda b, i: (w_layer, 0, 0)),
            rows(d, 0),
            _mod_spec(d, 2, layer, 2),
        ],
        out_specs=rows(d, 0),
        scratch_shapes=[pltpu.VMEM((tm, d_v), BF16)],
        compiler_params=_cparams("arbitrary", "arbitrary"),
        name="out_proj_hgrn2",
    )(o_dirs, o_dirs, p, norm_w.reshape(1, -1), w_out, xs, mods)


def _rot_columns(w):
    q = QK_ROPE // 4
    return jnp.concatenate([-w[..., q:2 * q], w[..., :q], -w[..., 3 * q:], w[..., 2 * q:3 * q]], axis=-1)


def _rope_tables(seq, n_ctx):
    t = np.arange(seq)
    row, colm = t // GRID_W, t % GRID_W
    half = QK_ROPE // 2
    inv_freq = ROPE_BASE ** (-np.arange(0, half, 2, dtype=np.float32) / half)
    ang_r = row.astype(np.float32)[:, None] * inv_freq
    ang_c = colm.astype(np.float32)[:, None] * inv_freq
    ang = jnp.asarray(np.concatenate([ang_r, ang_r, ang_c, ang_c], axis=-1), F32)
    cos = jnp.concatenate([jnp.cos(ang), jnp.ones((n_ctx, QK_ROPE), F32)], axis=0)
    sin = jnp.concatenate([jnp.sin(ang), jnp.zeros((n_ctx, QK_ROPE), F32)], axis=0)
    return cos, sin


def kernel(x, c, ctx, c_ctx, ada_w, ada_b, norm_mix_w, norm_mlp_w, ab_w_in, ab_w_out, lru_conv_w, lru_conv_b, lru_w_a, lru_b_a, lru_w_x, lru_b_x, lru_lambda, mla_q_norm_w, mla_w_uq, mla_kv_norm_w, mla_w_ukv, hg_w_in, hg_lb_logits, hg_norm_w, hg_w_out, mlp_w1, mlp_w2, final_norm_w):
    n_batch, seq, d = x.shape
    n_ctx = ctx.shape[1]
    depth = ada_w.shape[0]
    assert n_batch + 1 <= SUBLANES

    xs = jnp.concatenate([x, ctx], axis=1)
    cond8 = jnp.zeros((SUBLANES, d), F32).at[:n_batch].set(c).at[n_batch].set(c_ctx)
    mods = _ada_all(cond8, ada_w, ada_b)
    lb_all = _lower_bounds(hg_lb_logits)
    cos_t, sin_t = _rope_tables(seq, n_ctx)

    d_lru = lru_conv_w.shape[-1]
    q_lora = mla_q_norm_w.shape[-1]
    col_kr = ab_w_in.shape[2] - QK_ROPE
    ab_w_in_b = ab_w_in.astype(BF16)
    w_kr = ab_w_in[..., col_kr:]
    w_kr_ext = jnp.concatenate([w_kr, _rot_columns(w_kr)], axis=-1).astype(BF16)
    tn_ab = 1024 if col_kr % 1024 == 0 else col_kr // 3
    assert tn_ab % LANES == 0 and col_kr % tn_ab == 0
    d_hg = hg_lb_logits.shape[-1]
    hg_w_in_b = hg_w_in.astype(BF16)
    tn_hg = 1024 if d_hg % 1024 == 0 else d_hg
    ab_w_out_b = ab_w_out.astype(BF16)
    hg_w_out_b = hg_w_out.astype(BF16)
    w1_b = mlp_w1.astype(BF16)
    w2_b = mlp_w2.astype(BF16)
    wq = mla_w_uq.reshape(-1, q_lora, MLA_HEADS, QK_NOPE + QK_ROPE)
    wq_ext = jnp.concatenate([wq, _rot_columns(wq[..., QK_NOPE:])], axis=-1)
    wq_ext = wq_ext.reshape(wq.shape[0], q_lora, -1).astype(BF16)
    wkv_b = mla_w_ukv.astype(BF16)
    lru_w_a_b = lru_w_a.astype(BF16)
    lru_w_x_b = lru_w_x.astype(BF16)

    for l in range(depth):
        if l % 2 == 0:
            e = l // 2
            p, p_kr = _in_proj(xs, norm_mix_w[l], mods, ab_w_in_b, l, e, seq=seq, tn=tn_ab,
                               n_cols=col_kr, w_extra=w_kr_ext)
            ya = _lru(p, lru_conv_w[e], lru_conv_b[e], lru_w_a_b[e], lru_b_a[e],
                      lru_w_x_b[e], lru_b_x[e], lru_lambda[e], seq=seq)
            q, k, vt = _mla_qkv(p, p_kr, mla_q_norm_w[e], mla_kv_norm_w[e], wq_ext[e], wkv_b[e],
                                cos_t, sin_t, seq=seq, col0=2 * d_lru)
            yb = _attention(q, k, vt, seq=seq)
            xs = _out_proj_ab(ya, yb, ab_w_out_b, xs, mods, l, e, seq=seq)
        else:
            o = l // 2
            p_qig, p_f = _in_proj(xs, norm_mix_w[l], mods, hg_w_in_b, l, o, seq=seq, tn=tn_hg,
                                  f32_cols=(d_hg, 3 * d_hg))
            o_dirs = _hgrn2(p_qig, p_f, lb_all[l], seq=seq)
            xs = _out_proj_hg(o_dirs, p_qig, hg_norm_w[o], hg_w_out_b, xs, mods, l, o, seq=seq)
        last = l == depth - 1
        xs = _mlp(xs, norm_mlp_w[l], mods, w1_b, w2_b, l, seq=seq,
                  final_norm_w=final_norm_w if last else None)
    return xs
```
